```python
import math
import jax, jax.numpy as jnp
from jax import lax
import numpy as np

D_MODEL = 4096
BATCH = 4
SEQ = 2048
DEPTH = 1

A_KDIM = 128
A_VDIM = 128
A_WIDTH = D_MODEL // 2
A_HEADS = A_WIDTH // A_VDIM
A_KWIDTH = A_HEADS * A_KDIM
CHUNK = 64

B_HEAD_DIM = 128
B_WIDTH = D_MODEL // 2
B_HEADS = B_WIDTH // (2 * B_HEAD_DIM)
B_QK_WIDTH = B_HEADS * 2 * B_HEAD_DIM
Q_BLOCK = 128

NORM_EPS = 1e-6
SUBLN_EPS = 1e-5
NEG_INF = -1e30

kernel_name = "hgrn2_diffattn_gated_hybrid"


def rmsnorm(x, w, eps=NORM_EPS):
    xf = x.astype(jnp.float32)
    y = xf * lax.rsqrt(jnp.mean(xf * xf, axis=-1, keepdims=True) + eps) * w.astype(jnp.float32)
    return y.astype(x.dtype)


def hgrn2_mix(q, f_logit, i, lb):
    bsz, seq, _ = q.shape
    n_chunks = seq // CHUNK
    z = f_logit.astype(jnp.float32)
    lb = lb.astype(jnp.float32)
    log_f = jnp.log(lb + (1.0 - lb) * jax.nn.sigmoid(z))
    k = (1.0 - lb) * jax.nn.sigmoid(-z)

    def heads(t, d):
        return t.astype(jnp.float32).reshape(bsz, n_chunks, CHUNK, A_HEADS, d).transpose(0, 3, 1, 2, 4)

    qh, kh, lfh = heads(q, A_KDIM), heads(k, A_KDIM), heads(log_f, A_KDIM)
    vh = heads(i, A_VDIM)
    b = jnp.cumsum(lfh, axis=3)
    q_dec = qh * jnp.exp(b)
    k_inv = kh * jnp.exp(-b)
    causal = jnp.tril(jnp.ones((CHUNK, CHUNK), dtype=bool))
    scores = jnp.einsum('bhnck,bhnsk->bhncs', q_dec, k_inv)
    scores = jnp.where(causal, scores, 0.0)
    o_intra = jnp.einsum('bhncs,bhnsv->bhncv', scores, vh)

    b_last = b[:, :, :, -1:, :]
    k_to_end = kh * jnp.exp(b_last - b)
    chunk_state = jnp.einsum('bhnsk,bhnsv->bhnkv', k_to_end, vh)
    chunk_decay = jnp.exp(b_last[:, :, :, 0, :])

    def step(s_prev, inp):
        decay, upd = inp
        return decay[..., None] * s_prev + upd, s_prev

    s0 = jnp.zeros((bsz, A_HEADS, A_KDIM, A_VDIM), jnp.float32)
    _, s_in = lax.scan(step, s0, (chunk_decay.transpose(2, 0, 1, 3),
                                  chunk_state.transpose(2, 0, 1, 3, 4)))
    s_in = s_in.transpose(1, 2, 0, 3, 4)
    o_inter = jnp.einsum('bhnck,bhnkv->bhncv', q_dec, s_in)
    o = o_intra + o_inter
    return o.transpose(0, 2, 3, 1, 4).reshape(bsz, seq, A_HEADS, A_VDIM)


def diff_attention(q, k, v, lam, slopes):
    bsz, n_heads, _, seq, hd = q.shape
    n_blocks = seq // Q_BLOCK
    scale = hd ** -0.5
    q_blocks = q.reshape(bsz, n_heads, 2, n_blocks, Q_BLOCK, hd).transpose(3, 0, 1, 2, 4, 5)
    k_pos = jnp.arange(seq)

    def block(args):
        q_blk, blk_idx = args
        q_pos = blk_idx * Q_BLOCK + jnp.arange(Q_BLOCK)
        s = jnp.einsum('bhiqd,bhisd->bhiqs', q_blk, k).astype(jnp.float32) * scale
        dist = q_pos[:, None] - k_pos[None, :]
        bias = -slopes.astype(jnp.float32)[:, None, None] * dist.astype(jnp.float32)
        s = jnp.where(dist >= 0, s + bias[None, :, None], NEG_INF)
        p = jax.nn.softmax(s, axis=-1)
        a = p[:, :, 0] - lam * p[:, :, 1]
        return jnp.einsum('bhqs,bhsv->bhqv', a.astype(v.dtype), v)

    out = lax.map(block, (q_blocks, jnp.arange(n_blocks)))
    return out.transpose(1, 0, 3, 2, 4).reshape(bsz, seq, n_heads, 2 * hd)


def setup_inputs(seed: int = 0) -> dict:
    key = jax.random.key(seed)
    ks = jax.random.split(key, 16)
    n_in = 2 * A_KWIDTH + 2 * A_WIDTH + 2 * B_QK_WIDTH + 2 * B_WIDTH + 2 * D_MODEL
    f32 = jnp.float32
    return {
        "x": jax.random.normal(ks[0], (BATCH, SEQ, D_MODEL), f32),
        "norm_w": 1.0 + 0.02 * jax.random.normal(ks[1], (DEPTH, D_MODEL), f32),
        "w_in": jax.random.normal(ks[2], (DEPTH, D_MODEL, n_in), f32) * D_MODEL ** -0.5,
        "lower_bound_table": 1.0 + 0.1 * jax.random.normal(ks[3], (DEPTH + 1, A_KWIDTH), f32),
        "hgrn_norm_w": 1.0 + 0.02 * jax.random.normal(ks[4], (DEPTH, A_VDIM), f32),
        "lambda_q1": 0.1 * jax.random.normal(ks[5], (DEPTH, B_HEAD_DIM), f32),
        "lambda_k1": 0.1 * jax.random.normal(ks[6], (DEPTH, B_HEAD_DIM), f32),
        "lambda_q2": 0.1 * jax.random.normal(ks[7], (DEPTH, B_HEAD_DIM), f32),
        "lambda_k2": 0.1 * jax.random.normal(ks[8], (DEPTH, B_HEAD_DIM), f32),
        "subln_w": 1.0 + 0.02 * jax.random.normal(ks[9], (DEPTH, 2 * B_HEAD_DIM), f32),
        "w_branch_a": jax.random.normal(ks[10], (DEPTH, A_WIDTH, D_MODEL), f32) * A_WIDTH ** -0.5,
        "w_branch_b": jax.random.normal(ks[11], (DEPTH, B_WIDTH, D_MODEL), f32) * B_WIDTH ** -0.5,
        "w_out": jax.random.normal(ks[12], (DEPTH, D_MODEL, D_MODEL), f32) * D_MODEL ** -0.5,
        "final_w": 1.0 + 0.02 * jax.random.normal(ks[13], (D_MODEL,), f32),
    }


def reference(x, norm_w, w_in, lower_bound_table, hgrn_norm_w, lambda_q1, lambda_k1,
              lambda_q2, lambda_k2, subln_w, w_branch_a, w_branch_b, w_out, final_w):
    bsz, seq, _ = x.shape
    sizes = [A_KWIDTH, A_KWIDTH, A_WIDTH, A_WIDTH,
             B_QK_WIDTH, B_QK_WIDTH, B_WIDTH, B_WIDTH, D_MODEL, D_MODEL]
    split_idx = [int(s) for s in np.cumsum(sizes)[:-1]]
    lb_all = jnp.cumsum(jax.nn.softmax(lower_bound_table.astype(jnp.float32), axis=0), axis=0)
    slopes = jnp.exp2(-8.0 * (jnp.arange(B_HEADS, dtype=jnp.float32) + 1.0) / B_HEADS)

    h = x
    for l in range(DEPTH):
        u = rmsnorm(h, norm_w[l])
        proj = u @ w_in[l]
        a_q, a_f, a_i, a_g, b_q, b_k, b_v, b_g, gate_a, gate_b = jnp.split(proj, split_idx, axis=-1)

        o_a = hgrn2_mix(a_q, a_f, a_i, lb_all[l]).astype(u.dtype)
        o_a = rmsnorm(o_a, hgrn_norm_w[l]).reshape(bsz, seq, A_WIDTH) * jax.nn.silu(a_g)

        lam_init = 0.8 - 0.6 * math.exp(-0.3 * l)
        lam = (jnp.exp(jnp.sum(lambda_q1[l].astype(jnp.float32) * lambda_k1[l].astype(jnp.float32)))
               - jnp.exp(jnp.sum(lambda_q2[l].astype(jnp.float32) * lambda_k2[l].astype(jnp.float32)))
               + lam_init)
        qb = b_q.reshape(bsz, seq, B_HEADS, 2, B_HEAD_DIM).transpose(0, 2, 3, 1, 4)
        kb = b_k.reshape(bsz, seq, B_HEADS, 2, B_HEAD_DIM).transpose(0, 2, 3, 1, 4)
        vb = b_v.reshape(bsz, seq, B_HEADS, 2 * B_HEAD_DIM).transpose(0, 2, 1, 3)
        o_b = diff_attention(qb, kb, vb, lam, slopes)
        o_b = rmsnorm(o_b, subln_w[l], SUBLN_EPS) * (1.0 - lam_init)
        o_b = o_b.reshape(bsz, seq, B_WIDTH) * jax.nn.silu(b_g)

        y = (jax.nn.sigmoid(gate_a) * (o_a @ w_branch_a[l])
             + jax.nn.sigmoid(gate_b) * (o_b @ w_branch_b[l]))
        h = h + y @ w_out[l]
    return rmsnorm(h, final_w)
```

```python
import functools
import math

import jax
import jax.numpy as jnp
from jax import lax
from jax.experimental import pallas as pl
from jax.experimental.pallas import tpu as pltpu

D_MODEL = 4096
DEPTH = 1
assert DEPTH == 1, "the final rmsnorm is fused into the (single) layer's output projection"

A_KDIM = 128
A_VDIM = 128
A_WIDTH = D_MODEL // 2
A_HEADS = A_WIDTH // A_VDIM
A_KWIDTH = A_HEADS * A_KDIM
CHUNK = 64

B_HEAD_DIM = 128
B_WIDTH = D_MODEL // 2
B_HEADS = B_WIDTH // (2 * B_HEAD_DIM)
B_QK_WIDTH = B_HEADS * 2 * B_HEAD_DIM

NORM_EPS = 1e-6
SUBLN_EPS = 1e-5
NEG_INF = -1e30

F32 = jnp.float32
BF16 = jnp.bfloat16

V7X_VMEM_LIMIT_BYTES = 56 * 1024 * 1024

OFF_A_Q = 0
OFF_A_I = OFF_A_Q + A_KWIDTH
OFF_A_G = OFF_A_I + A_WIDTH
OFF_B_Q = OFF_A_G + A_WIDTH
OFF_B_K = OFF_B_Q + B_QK_WIDTH
OFF_B_V = OFF_B_K + B_QK_WIDTH
OFF_B_G = OFF_B_V + B_WIDTH
OFF_GATE_A = OFF_B_G + B_WIDTH
OFF_GATE_B = OFF_GATE_A + D_MODEL
P_WIDTH = OFF_GATE_B + D_MODEL
N_IN = P_WIDTH + A_KWIDTH


def _params(n_axes):
    return pltpu.CompilerParams(
        dimension_semantics=("arbitrary",) * n_axes,
        vmem_limit_bytes=V7X_VMEM_LIMIT_BYTES,
    )


def _rmsnorm_cast_kernel(x_ref, w_ref, o_ref):
    x = x_ref[...]
    ms = jnp.mean(x * x, axis=-1, keepdims=True)
    o_ref[...] = (x * lax.rsqrt(ms + NORM_EPS) * w_ref[...]).astype(o_ref.dtype)


def _rmsnorm_cast(x2d, w, rows=256):
    t, d = x2d.shape
    return pl.pallas_call(
        _rmsnorm_cast_kernel,
        grid=(t // rows,),
        in_specs=[pl.BlockSpec((rows, d), lambda i: (i, 0)),
                  pl.BlockSpec((1, d), lambda i: (0, 0))],
        out_specs=pl.BlockSpec((rows, d), lambda i: (i, 0)),
        out_shape=jax.ShapeDtypeStruct((t, d), BF16),
        compiler_params=_params(1),
        name="rmsnorm_cast",
    )(x2d, w.reshape(1, d))


def _proj_kernel(u_ref, w_ref, o_ref, wb_ref):
    @pl.when(pl.program_id(1) == 0)
    def _():
        wb_ref[...] = w_ref[...].astype(BF16)

    o_ref[...] = jnp.dot(u_ref[...], wb_ref[...],
                         preferred_element_type=F32).astype(o_ref.dtype)


def _proj(u, w, col_map, n_cols, out_dtype, tm=1024, tn=512, name="proj"):
    t, d = u.shape
    return pl.pallas_call(
        _proj_kernel,
        grid=(n_cols // tn, t // tm),
        in_specs=[pl.BlockSpec((tm, d), lambda j, i: (i, 0)),
                  pl.BlockSpec((d, tn), lambda j, i: (0, col_map(j)))],
        out_specs=pl.BlockSpec((tm, tn), lambda j, i: (i, j)),
        out_shape=jax.ShapeDtypeStruct((t, n_cols), out_dtype),
        scratch_shapes=[pltpu.VMEM((d, tn), BF16)],
        compiler_params=_params(2),
        name=name,
    )(u, w)


def _hgrn_kernel(q_ref, z_ref, i_ref, g_ref, lbt_ref, nw_ref, o_ref, *, layer, n_chunks):
    t = lbt_ref[...]
    e = jnp.exp(t - jnp.max(t, axis=0, keepdims=True))
    lb = jnp.sum(e[:layer + 1], axis=0, keepdims=True) / jnp.sum(e, axis=0, keepdims=True)
    one_m_lb = 1.0 - lb
    nw = nw_ref[...]

    row = lax.broadcasted_iota(jnp.int32, (CHUNK, CHUNK), 0)
    col = lax.broadcasted_iota(jnp.int32, (CHUNK, CHUNK), 1)
    causal = row >= col
    tril = causal.astype(BF16)

    def body(n, st):
        r0 = pl.multiple_of(n * CHUNK, CHUNK)
        z = z_ref[pl.ds(r0, CHUNK), :]
        ez = jnp.exp(-jnp.abs(z))
        r = 1.0 / (1.0 + ez)
        pos = z >= 0.0
        sig_p = jnp.where(pos, r, ez * r)
        sig_n = jnp.where(pos, ez * r, r)
        log_f = jnp.log(lb + one_m_lb * sig_p)
        kk = one_m_lb * sig_n

        h1 = log_f.astype(BF16)
        r1 = log_f - h1.astype(F32)
        h2 = r1.astype(BF16)
        h3 = (r1 - h2.astype(F32)).astype(BF16)
        b = (jnp.dot(tril, h1, preferred_element_type=F32)
             + jnp.dot(tril, h2, preferred_element_type=F32)
             + jnp.dot(tril, h3, preferred_element_type=F32))
        b_last = b[CHUNK - 1:CHUNK, :]

        q = q_ref[pl.ds(r0, CHUNK), :].astype(F32)
        v = i_ref[pl.ds(r0, CHUNK), :]
        q_dec = (q * jnp.exp(b)).astype(BF16)
        k_inv = (kk * jnp.exp(-b)).astype(BF16)
        k_end = (kk * jnp.exp(b_last - b)).astype(BF16)

        scores = lax.dot_general(q_dec, k_inv, (((1,), (1,)), ((), ())),
                                 preferred_element_type=F32)
        scores = jnp.where(causal, scores, 0.0).astype(BF16)
        o = jnp.dot(scores, v, preferred_element_type=F32)
        o = o + lax.dot_general(q_dec, st.astype(BF16), (((1,), (1,)), ((), ())),
                                preferred_element_type=F32)

        upd = lax.dot_general(v, k_end, (((0,), (0,)), ((), ())),
                              preferred_element_type=F32)
        st_new = st * jnp.exp(b_last) + upd

        ms = jnp.mean(o * o, axis=-1, keepdims=True)
        on = o * lax.rsqrt(ms + NORM_EPS) * nw
        g = g_ref[pl.ds(r0, CHUNK), :].astype(F32)
        o_ref[pl.ds(r0, CHUNK), :] = (on * (g * jax.nn.sigmoid(g))).astype(o_ref.dtype)
        return st_new

    lax.fori_loop(0, n_chunks, body, jnp.zeros((A_VDIM, A_KDIM), F32))


def _hgrn(p, f, lb_table, norm_w, bsz, seq, layer):
    t = bsz * seq
    kernel = functools.partial(_hgrn_kernel, layer=layer, n_chunks=seq // CHUNK)
    blk = lambda off: pl.BlockSpec((seq, A_KDIM), lambda b, h: (b, off // A_KDIM + h))
    return pl.pallas_call(
        kernel,
        grid=(bsz, A_HEADS),
        in_specs=[blk(OFF_A_Q),
                  pl.BlockSpec((seq, A_KDIM), lambda b, h: (b, h)),
                  blk(OFF_A_I),
                  blk(OFF_A_G),
                  pl.BlockSpec((DEPTH + 1, A_KDIM), lambda b, h: (0, h)),
                  pl.BlockSpec((1, A_VDIM), lambda b, h: (0, 0))],
        out_specs=pl.BlockSpec((seq, A_VDIM), lambda b, h: (b, h)),
        out_shape=jax.ShapeDtypeStruct((t, A_WIDTH), BF16),
        compiler_params=_params(2),
        name="hgrn2",
    )(p, f, p, p, lb_table, norm_w.reshape(1, A_VDIM))


def _attn_kernel(q_ref, k_ref, v_ref, g_ref, lq1_ref, lk1_ref, lq2_ref, lk2_ref, sw_ref,
                 o_ref, vt_ref, acc_ref, *, tq, tk, seq, lam_init):
    h = pl.program_id(1)
    qi = pl.program_id(2)
    hd = B_HEAD_DIM
    scale = hd ** -0.5

    @pl.when(qi == 0)
    def _():
        for c in range(seq // tk):
            vt_ref[c] = v_ref[c * tk:(c + 1) * tk, :].astype(F32).T.astype(BF16)

    lam = (jnp.exp(jnp.sum(lq1_ref[...] * lk1_ref[...], axis=-1, keepdims=True))
           - jnp.exp(jnp.sum(lq2_ref[...] * lk2_ref[...], axis=-1, keepdims=True))
           + lam_init)
    slope = jnp.exp2(jnp.full((1, 1), -8.0 / B_HEADS, F32) * (h + 1).astype(F32))

    k_loc = lax.broadcasted_iota(jnp.int32, (tk, tq), 0)
    q_loc = lax.broadcasted_iota(jnp.int32, (tk, tq), 1)
    rel = q_loc - k_loc
    bias = -slope * rel.astype(F32)
    causal = rel >= 0

    outs = []
    for mi in range(2):
        q = q_ref[:, mi * hd:(mi + 1) * hd]
        acc_ref[mi] = jnp.zeros((2 * hd, tq), F32)

        def step(kb, m, l, masked, q=q, mi=mi):
            k0 = pl.multiple_of(kb * tk, tk)
            kblk = k_ref[pl.ds(k0, tk), mi * hd:(mi + 1) * hd]
            s = lax.dot_general(kblk, q, (((1,), (1,)), ((), ())),
                                preferred_element_type=F32)
            s = s * scale + bias
            if masked:
                s = jnp.where(causal, s, NEG_INF)
            cb = -slope * (qi * tq - kb * tk).astype(F32)
            m_new = jnp.maximum(m, jnp.max(s, axis=0, keepdims=True) + cb)
            alpha = jnp.exp(m - m_new)
            p = jnp.exp(s - (m_new - cb))
            l_new = alpha * l + jnp.sum(p, axis=0, keepdims=True)
            pv = jnp.dot(vt_ref[kb], p.astype(BF16), preferred_element_type=F32)
            acc_ref[mi] = acc_ref[mi] * alpha + pv
            return m_new, l_new

        m0 = jnp.full((1, tq), NEG_INF, F32)
        l0 = jnp.zeros((1, tq), F32)
        m, l = lax.fori_loop(0, qi, lambda kb, c: step(kb, c[0], c[1], False), (m0, l0))
        m, l = step(qi, m, l, True)
        outs.append(acc_ref[mi] / l)

    ot = outs[0] - lam * outs[1]
    o = ot.T
    ms = jnp.mean(o * o, axis=-1, keepdims=True)
    on = o * lax.rsqrt(ms + SUBLN_EPS) * sw_ref[...] * (1.0 - lam_init)
    g = g_ref[...].astype(F32)
    o_ref[...] = (on * (g * jax.nn.sigmoid(g))).astype(o_ref.dtype)


def _attn(p, lq1, lk1, lq2, lk2, subln_w, bsz, seq, layer, tq=256, tk=256):
    t = bsz * seq
    nq = seq // tq
    w = 2 * B_HEAD_DIM
    lam_init = 0.8 - 0.6 * math.exp(-0.3 * layer)
    kernel = functools.partial(_attn_kernel, tq=tq, tk=tk, seq=seq, lam_init=lam_init)
    vec = lambda n: pl.BlockSpec((1, n), lambda b, h, i: (0, 0))
    return pl.pallas_call(
        kernel,
        grid=(bsz, B_HEADS, nq),
        in_specs=[pl.BlockSpec((tq, w), lambda b, h, i: (b * nq + i, OFF_B_Q // w + h)),
                  pl.BlockSpec((seq, w), lambda b, h, i: (b, OFF_B_K // w + h)),
                  pl.BlockSpec((seq, w), lambda b, h, i: (b, OFF_B_V // w + h)),
                  pl.BlockSpec((tq, w), lambda b, h, i: (b * nq + i, OFF_B_G // w + h)),
                  vec(B_HEAD_DIM), vec(B_HEAD_DIM), vec(B_HEAD_DIM), vec(B_HEAD_DIM),
                  vec(w)],
        out_specs=pl.BlockSpec((tq, w), lambda b, h, i: (b * nq + i, h)),
        out_shape=jax.ShapeDtypeStruct((t, B_WIDTH), BF16),
        scratch_shapes=[pltpu.VMEM((seq // tk, w, tk), BF16),
                        pltpu.VMEM((2, w, tq), F32)],
        compiler_params=_params(3),
        name="diff_attn",
    )(p, p, p, p, lq1.reshape(1, -1), lk1.reshape(1, -1), lq2.reshape(1, -1),
      lk2.reshape(1, -1), subln_w.reshape(1, -1))


def _merge_kernel(oa_ref, ob_ref, wa_ref, wb_ref, ga_ref, gb_ref, y_ref):
    ya = jnp.dot(oa_ref[...], wa_ref[...], preferred_element_type=F32)
    yb = jnp.dot(ob_ref[...], wb_ref[...], preferred_element_type=F32)
    y = (jax.nn.sigmoid(ga_ref[...].astype(F32)) * ya
         + jax.nn.sigmoid(gb_ref[...].astype(F32)) * yb)
    y_ref[...] = y.astype(y_ref.dtype)


def _merge(oa, ob, wa, wb, p, tm=512, tn=1024):
    t = oa.shape[0]
    return pl.pallas_call(
        _merge_kernel,
        grid=(D_MODEL // tn, t // tm),
        in_specs=[pl.BlockSpec((tm, A_WIDTH), lambda j, i: (i, 0)),
                  pl.BlockSpec((tm, B_WIDTH), lambda j, i: (i, 0)),
                  pl.BlockSpec((A_WIDTH, tn), lambda j, i: (0, j)),
                  pl.BlockSpec((B_WIDTH, tn), lambda j, i: (0, j)),
                  pl.BlockSpec((tm, tn), lambda j, i: (i, OFF_GATE_A // tn + j)),
                  pl.BlockSpec((tm, tn), lambda j, i: (i, OFF_GATE_B // tn + j))],
        out_specs=pl.BlockSpec((tm, tn), lambda j, i: (i, j)),
        out_shape=jax.ShapeDtypeStruct((t, D_MODEL), BF16),
        compiler_params=_params(2),
        name="gated_merge",
    )(oa, ob, wa, wb, p, p)


def _out_kernel(y_ref, w_ref, x_ref, fw_ref, o_ref, *, tn):
    j = pl.program_id(1)
    c0 = pl.multiple_of(j * tn, tn)
    o_ref[:, pl.ds(c0, tn)] = x_ref[...] + jnp.dot(y_ref[...], w_ref[...],
                                                   preferred_element_type=F32)

    @pl.when(j == pl.num_programs(1) - 1)
    def _():
        hres = o_ref[...]
        ms = jnp.mean(hres * hres, axis=-1, keepdims=True)
        o_ref[...] = hres * lax.rsqrt(ms + NORM_EPS) * fw_ref[...]


def _out_proj(y, w, x2d, final_w, tm=512, tn=512):
    t, d = x2d.shape
    return pl.pallas_call(
        functools.partial(_out_kernel, tn=tn),
        grid=(t // tm, d // tn),
        in_specs=[pl.BlockSpec((tm, d), lambda i, j: (i, 0)),
                  pl.BlockSpec((d, tn), lambda i, j: (0, j)),
                  pl.BlockSpec((tm, tn), lambda i, j: (i, j)),
                  pl.BlockSpec((1, d), lambda i, j: (0, 0))],
        out_specs=pl.BlockSpec((tm, d), lambda i, j: (i, 0)),
        out_shape=jax.ShapeDtypeStruct((t, d), F32),
        compiler_params=_params(2),
        name="out_proj_norm",
    )(y, w, x2d, final_w.reshape(1, d))


def kernel(x, norm_w, w_in, lower_bound_table, hgrn_norm_w, lambda_q1, lambda_k1,
           lambda_q2, lambda_k2, subln_w, w_branch_a, w_branch_b, w_out, final_w):
    bsz, seq, d = x.shape
    h2d = x.reshape(bsz * seq, d)
    for l in range(DEPTH):
        u = _rmsnorm_cast(h2d, norm_w[l])
        tn = 512
        f_blk0 = A_KWIDTH // tn
        f_blks = A_KWIDTH // tn
        f = _proj(u, w_in[l], lambda j: j + f_blk0, A_KWIDTH, F32, tn=tn, name="proj_f")
        p = _proj(u, w_in[l], lambda j: jnp.where(j >= f_blk0, j + f_blks, j), P_WIDTH,
                  BF16, tn=tn, name="proj")
        oa = _hgrn(p, f, lower_bound_table, hgrn_norm_w[l], bsz, seq, l)
        ob = _attn(p, lambda_q1[l], lambda_k1[l], lambda_q2[l], lambda_k2[l], subln_w[l],
                   bsz, seq, l)
        y = _merge(oa, ob, w_branch_a[l].astype(BF16), w_branch_b[l].astype(BF16), p)
        h2d = _out_proj(y, w_out[l].astype(BF16), h2d, final_w)
    return h2d.reshape(bsz, seq, d)
```

```python
import functools
import math

import jax
import jax.numpy as jnp
from jax import lax
from jax.experimental import pallas as pl
from jax.experimental.pallas import tpu as pltpu

D_MODEL = 4096
DEPTH = 1
assert DEPTH == 1, "the final rmsnorm is fused into the (single) layer's output projection"

A_KDIM = 128
A_VDIM = 128
A_WIDTH = D_MODEL // 2
A_HEADS = A_WIDTH // A_VDIM
A_KWIDTH = A_HEADS * A_KDIM
CHUNK = 64

B_HEAD_DIM = 128
B_WIDTH = D_MODEL // 2
B_HEADS = B_WIDTH // (2 * B_HEAD_DIM)
B_QK_WIDTH = B_HEADS * 2 * B_HEAD_DIM

NORM_EPS = 1e-6
SUBLN_EPS = 1e-5
NEG_INF = -1e30
LOG2E = math.log2(math.e)

F32 = jnp.float32
BF16 = jnp.bfloat16

V7X_VMEM_LIMIT_BYTES = 56 * 1024 * 1024

OFF_A_Q = 0
OFF_A_I = OFF_A_Q + A_KWIDTH
OFF_A_G = OFF_A_I + A_WIDTH
OFF_B_Q = OFF_A_G + A_WIDTH
OFF_B_K = OFF_B_Q + B_QK_WIDTH
OFF_B_V = OFF_B_K + B_QK_WIDTH
OFF_B_G = OFF_B_V + B_WIDTH
OFF_GATE_A = OFF_B_G + B_WIDTH
OFF_GATE_B = OFF_GATE_A + D_MODEL
P_WIDTH = OFF_GATE_B + D_MODEL
N_IN = P_WIDTH + A_KWIDTH

NT_DIMS = (((1,), (1,)), ((), ()))
TN_DIMS = (((0,), (0,)), ((), ()))


def _params(n_axes):
    return pltpu.CompilerParams(
        dimension_semantics=("arbitrary",) * n_axes,
        vmem_limit_bytes=V7X_VMEM_LIMIT_BYTES,
    )


def _rmsnorm_cast_kernel(x_ref, w_ref, o_ref):
    x = x_ref[...]
    ms = jnp.mean(x * x, axis=-1, keepdims=True)
    o_ref[...] = (x * lax.rsqrt(ms + NORM_EPS) * w_ref[...]).astype(o_ref.dtype)


def _rmsnorm_cast(x2d, w, rows=256):
    t, d = x2d.shape
    return pl.pallas_call(
        _rmsnorm_cast_kernel,
        grid=(t // rows,),
        in_specs=[pl.BlockSpec((rows, d), lambda i: (i, 0)),
                  pl.BlockSpec((1, d), lambda i: (0, 0))],
        out_specs=pl.BlockSpec((rows, d), lambda i: (i, 0)),
        out_shape=jax.ShapeDtypeStruct((t, d), BF16),
        compiler_params=_params(1),
        name="rmsnorm_cast",
    )(x2d, w.reshape(1, d))


def _proj_kernel(u_ref, w_ref, o_ref, wb_ref):
    @pl.when(pl.program_id(1) == 0)
    def _():
        wb_ref[...] = w_ref[...].astype(BF16)

    o_ref[...] = jnp.dot(u_ref[...], wb_ref[...],
                         preferred_element_type=F32).astype(o_ref.dtype)


def _proj(u, w, col_map, n_cols, out_dtype, tm=1024, tn=512, name="proj"):
    t, d = u.shape
    return pl.pallas_call(
        _proj_kernel,
        grid=(n_cols // tn, t // tm),
        in_specs=[pl.BlockSpec((tm, d), lambda j, i: (i, 0)),
                  pl.BlockSpec((d, tn), lambda j, i: (0, col_map(j)))],
        out_specs=pl.BlockSpec((tm, tn), lambda j, i: (i, j)),
        out_shape=jax.ShapeDtypeStruct((t, n_cols), out_dtype),
        scratch_shapes=[pltpu.VMEM((d, tn), BF16)],
        compiler_params=_params(2),
        name=name,
    )(u, w)


def _hgrn_kernel(q_ref, z_ref, i_ref, g_ref, lbt_ref, nw_ref, o_ref, *, layer, n_groups, group):
    rows = group * CHUNK
    t = lbt_ref[...]
    e = jnp.exp(t - jnp.max(t, axis=0, keepdims=True))
    lb = jnp.sum(e[:layer + 1], axis=0, keepdims=True) / jnp.sum(e, axis=0, keepdims=True)
    one_m_lb = 1.0 - lb
    nw = nw_ref[...]

    row = lax.broadcasted_iota(jnp.int32, (CHUNK, CHUNK), 0)
    col = lax.broadcasted_iota(jnp.int32, (CHUNK, CHUNK), 1)
    causal = row >= col
    tril = causal.astype(BF16)

    def body(n, st):
        r0 = pl.multiple_of(n * rows, rows)
        z = z_ref[pl.ds(r0, rows), :]
        ez = jnp.exp(-jnp.abs(z))
        r = 1.0 / (1.0 + ez)
        pos = z >= 0.0
        sig_p = jnp.where(pos, r, ez * r)
        sig_n = jnp.where(pos, ez * r, r)
        log_f = jnp.log(lb + one_m_lb * sig_p)
        kk = one_m_lb * sig_n

        h1 = log_f.astype(BF16)
        r1 = log_f - h1.astype(F32)
        h2 = r1.astype(BF16)
        h3 = (r1 - h2.astype(F32)).astype(BF16)
        hs = jnp.concatenate([h1, h2, h3], axis=1)
        bs = []
        for c in range(group):
            t3 = jnp.dot(tril, hs[c * CHUNK:(c + 1) * CHUNK], preferred_element_type=F32)
            bs.append(t3[:, :A_KDIM] + t3[:, A_KDIM:2 * A_KDIM] + t3[:, 2 * A_KDIM:])
        b = jnp.concatenate(bs, axis=0)

        eb = jnp.exp(b)
        q_dec = (q_ref[pl.ds(r0, rows), :].astype(F32) * eb).astype(BF16)
        k_inv32 = kk * (1.0 / eb)
        k_inv = k_inv32.astype(BF16)
        v = i_ref[pl.ds(r0, rows), :]

        o_intra, upd, decay = [], [], []
        for c in range(group):
            sl = slice(c * CHUNK, (c + 1) * CHUNK)
            scores = lax.dot_general(q_dec[sl], k_inv[sl], NT_DIMS, preferred_element_type=F32)
            scores = jnp.where(causal, scores, 0.0).astype(BF16)
            o_intra.append(jnp.dot(scores, v[sl], preferred_element_type=F32))
            d_c = jnp.exp(bs[c][CHUNK - 1:CHUNK, :])
            k_end = (k_inv32[sl] * d_c).astype(BF16)
            upd.append(lax.dot_general(v[sl], k_end, TN_DIMS, preferred_element_type=F32))
            decay.append(d_c)

        outs = []
        for c in range(group):
            sl = slice(c * CHUNK, (c + 1) * CHUNK)
            outs.append(o_intra[c] + lax.dot_general(q_dec[sl], st.astype(BF16), NT_DIMS,
                                                     preferred_element_type=F32))
            st = st * decay[c] + upd[c]
        o = jnp.concatenate(outs, axis=0)

        ms = jnp.mean(o * o, axis=-1, keepdims=True)
        on = o * lax.rsqrt(ms + NORM_EPS) * nw
        g = g_ref[pl.ds(r0, rows), :].astype(F32)
        o_ref[pl.ds(r0, rows), :] = (on * (g * jax.nn.sigmoid(g))).astype(o_ref.dtype)
        return st

    lax.fori_loop(0, n_groups, body, jnp.zeros((A_VDIM, A_KDIM), F32))


def _hgrn(p, f, lb_table, norm_w, bsz, seq, layer, group=8):
    t = bsz * seq
    kernel = functools.partial(_hgrn_kernel, layer=layer, n_groups=seq // (CHUNK * group),
                               group=group)
    blk = lambda off: pl.BlockSpec((seq, A_KDIM), lambda b, h: (b, off // A_KDIM + h))
    return pl.pallas_call(
        kernel,
        grid=(bsz, A_HEADS),
        in_specs=[blk(OFF_A_Q),
                  pl.BlockSpec((seq, A_KDIM), lambda b, h: (b, h)),
                  blk(OFF_A_I),
                  blk(OFF_A_G),
                  pl.BlockSpec((DEPTH + 1, A_KDIM), lambda b, h: (0, h)),
                  pl.BlockSpec((1, A_VDIM), lambda b, h: (0, 0))],
        out_specs=pl.BlockSpec((seq, A_VDIM), lambda b, h: (b, h)),
        out_shape=jax.ShapeDtypeStruct((t, A_WIDTH), BF16),
        compiler_params=_params(2),
        name="hgrn2",
    )(p, f, p, p, lb_table, norm_w.reshape(1, A_VDIM))


def _attn_kernel(q_ref, k_ref, v_ref, g_ref, lq1_ref, lk1_ref, lq2_ref, lk2_ref, sw_ref,
                 o_ref, vt_ref, acc_ref, *, tq, tk, seq, lam_init, hpb):
    hb = pl.program_id(1)
    qi = pl.program_id(2)
    hd = B_HEAD_DIM
    w = 2 * hd
    c2 = hd ** -0.5 * LOG2E

    @pl.when(qi == 0)
    def _():
        for hh in range(hpb):
            for c in range(seq // tk):
                vt_ref[hh, c] = (v_ref[c * tk:(c + 1) * tk, hh * w:(hh + 1) * w]
                                 .astype(F32).T.astype(BF16))

    lam = (jnp.exp(jnp.sum(lq1_ref[...] * lk1_ref[...], axis=-1, keepdims=True))
           - jnp.exp(jnp.sum(lq2_ref[...] * lk2_ref[...], axis=-1, keepdims=True))
           + lam_init)

    k_loc = lax.broadcasted_iota(jnp.int32, (tk, tq), 0)
    q_loc = lax.broadcasted_iota(jnp.int32, (tk, tq), 1)
    rel = q_loc - k_loc
    causal = rel >= 0
    rel_f = rel.astype(F32)
    slope2 = [jnp.exp2(jnp.full((1, 1), -8.0 / B_HEADS, F32)
                       * (hb * hpb + hh + 1).astype(F32)) * LOG2E for hh in range(hpb)]
    bias2 = [-s2 * rel_f for s2 in slope2]

    chains = [(hh, mi) for hh in range(hpb) for mi in range(2)]
    qs = [q_ref[:, hh * w + mi * hd: hh * w + (mi + 1) * hd] for hh, mi in chains]
    for ci in range(len(chains)):
        acc_ref[ci] = jnp.zeros((w, tq), F32)

    def step(kb, carry, masked):
        k0 = pl.multiple_of(kb * tk, tk)
        blk_dist = (qi * tq - kb * tk).astype(F32)
        new = []
        for ci, (hh, mi) in enumerate(chains):
            m, l = carry[2 * ci], carry[2 * ci + 1]
            kblk = k_ref[pl.ds(k0, tk), hh * w + mi * hd: hh * w + (mi + 1) * hd]
            s = lax.dot_general(kblk, qs[ci], NT_DIMS, preferred_element_type=F32)
            s = s * c2 + bias2[hh]
            if masked:
                s = jnp.where(causal, s, NEG_INF)
            cb = -slope2[hh] * blk_dist
            m_new = jnp.maximum(m, jnp.max(s, axis=0, keepdims=True) + cb)
            alpha = jnp.exp2(m - m_new)
            p = jnp.exp2(s - (m_new - cb))
            l_new = alpha * l + jnp.sum(p, axis=0, keepdims=True)
            pv = jnp.dot(vt_ref[hh, kb], p.astype(BF16), preferred_element_type=F32)
            acc_ref[ci] = acc_ref[ci] * alpha + pv
            new += [m_new, l_new]
        return tuple(new)

    init = (jnp.full((1, tq), NEG_INF, F32), jnp.zeros((1, tq), F32)) * len(chains)
    carry = lax.fori_loop(0, qi, lambda kb, c: step(kb, c, False), init)
    carry = step(qi, carry, True)

    g = g_ref[...].astype(F32)
    gate = g * jax.nn.sigmoid(g)
    for hh in range(hpb):
        l0, l1 = carry[4 * hh + 1], carry[4 * hh + 3]
        ot = acc_ref[2 * hh] * (1.0 / l0) - acc_ref[2 * hh + 1] * (lam * (1.0 / l1))
        o = ot.T
        ms = jnp.mean(o * o, axis=-1, keepdims=True)
        on = o * lax.rsqrt(ms + SUBLN_EPS) * sw_ref[...] * (1.0 - lam_init)
        o_ref[:, hh * w:(hh + 1) * w] = (on * gate[:, hh * w:(hh + 1) * w]).astype(o_ref.dtype)


def _attn(p, lq1, lk1, lq2, lk2, subln_w, bsz, seq, layer, tq=256, tk=256, hpb=4):
    t = bsz * seq
    nq = seq // tq
    w = 2 * B_HEAD_DIM
    bw = hpb * w
    lam_init = 0.8 - 0.6 * math.exp(-0.3 * layer)
    kernel = functools.partial(_attn_kernel, tq=tq, tk=tk, seq=seq, lam_init=lam_init, hpb=hpb)
    vec = lambda n: pl.BlockSpec((1, n), lambda b, h, i: (0, 0))
    return pl.pallas_call(
        kernel,
        grid=(bsz, B_HEADS // hpb, nq),
        in_specs=[pl.BlockSpec((tq, bw), lambda b, h, i: (b * nq + i, OFF_B_Q // bw + h)),
                  pl.BlockSpec((seq, bw), lambda b, h, i: (b, OFF_B_K // bw + h)),
                  pl.BlockSpec((seq, bw), lambda b, h, i: (b, OFF_B_V // bw + h)),
                  pl.BlockSpec((tq, bw), lambda b, h, i: (b * nq + i, OFF_B_G // bw + h)),
                  vec(B_HEAD_DIM), vec(B_HEAD_DIM), vec(B_HEAD_DIM), vec(B_HEAD_DIM),
                  vec(w)],
        out_specs=pl.BlockSpec((tq, bw), lambda b, h, i: (b * nq + i, h)),
        out_shape=jax.ShapeDtypeStruct((t, B_WIDTH), BF16),
        scratch_shapes=[pltpu.VMEM((hpb, seq // tk, w, tk), BF16),
                        pltpu.VMEM((2 * hpb, w, tq), F32)],
        compiler_params=_params(3),
        name="diff_attn",
    )(p, p, p, p, lq1.reshape(1, -1), lk1.reshape(1, -1), lq2.reshape(1, -1),
      lk2.reshape(1, -1), subln_w.reshape(1, -1))


def _merge_kernel(oa_ref, ob_ref, wa_ref, wb_ref, ga_ref, gb_ref, y_ref):
    ya = jnp.dot(oa_ref[...], wa_ref[...], preferred_element_type=F32)
    yb = jnp.dot(ob_ref[...], wb_ref[...], preferred_element_type=F32)
    y = (jax.nn.sigmoid(ga_ref[...].astype(F32)) * ya
         + jax.nn.sigmoid(gb_ref[...].astype(F32)) * yb)
    y_ref[...] = y.astype(y_ref.dtype)


def _merge(oa, ob, wa, wb, p, tm=512, tn=1024):
    t = oa.shape[0]
    return pl.pallas_call(
        _merge_kernel,
        grid=(D_MODEL // tn, t // tm),
        in_specs=[pl.BlockSpec((tm, A_WIDTH), lambda j, i: (i, 0)),
                  pl.BlockSpec((tm, B_WIDTH), lambda j, i: (i, 0)),
                  pl.BlockSpec((A_WIDTH, tn), lambda j, i: (0, j)),
                  pl.BlockSpec((B_WIDTH, tn), lambda j, i: (0, j)),
                  pl.BlockSpec((tm, tn), lambda j, i: (i, OFF_GATE_A // tn + j)),
                  pl.BlockSpec((tm, tn), lambda j, i: (i, OFF_GATE_B // tn + j))],
        out_specs=pl.BlockSpec((tm, tn), lambda j, i: (i, j)),
        out_shape=jax.ShapeDtypeStruct((t, D_MODEL), BF16),
        compiler_params=_params(2),
        name="gated_merge",
    )(oa, ob, wa, wb, p, p)


def _out_kernel(y_ref, w_ref, x_ref, fw_ref, o_ref, *, tn):
    j = pl.program_id(1)
    c0 = pl.multiple_of(j * tn, tn)
    o_ref[:, pl.ds(c0, tn)] = x_ref[...] + jnp.dot(y_ref[...], w_ref[...],
                                                   preferred_element_type=F32)

    @pl.when(j == pl.num_programs(1) - 1)
    def _():
        hres = o_ref[...]
        ms = jnp.mean(hres * hres, axis=-1, keepdims=True)
        o_ref[...] = hres * lax.rsqrt(ms + NORM_EPS) * fw_ref[...]


def _out_proj(y, w, x2d, final_w, tm=512, tn=512):
    t, d = x2d.shape
    return pl.pallas_call(
        functools.partial(_out_kernel, tn=tn),
        grid=(t // tm, d // tn),
        in_specs=[pl.BlockSpec((tm, d), lambda i, j: (i, 0)),
                  pl.BlockSpec((d, tn), lambda i, j: (0, j)),
                  pl.BlockSpec((tm, tn), lambda i, j: (i, j)),
                  pl.BlockSpec((1, d), lambda i, j: (0, 0))],
        out_specs=pl.BlockSpec((tm, d), lambda i, j: (i, 0)),
        out_shape=jax.ShapeDtypeStruct((t, d), F32),
        compiler_params=_params(2),
        name="out_proj_norm",
    )(y, w, x2d, final_w.reshape(1, d))


def kernel(x, norm_w, w_in, lower_bound_table, hgrn_norm_w, lambda_q1, lambda_k1,
           lambda_q2, lambda_k2, subln_w, w_branch_a, w_branch_b, w_out, final_w):
    bsz, seq, d = x.shape
    h2d = x.reshape(bsz * seq, d)
    for l in range(DEPTH):
        u = _rmsnorm_cast(h2d, norm_w[l])
        tn = 512
        f_blk0 = A_KWIDTH // tn
        f_blks = A_KWIDTH // tn
        f = _proj(u, w_in[l], lambda j: j + f_blk0, A_KWIDTH, F32, tn=tn, name="proj_f")
        p = _proj(u, w_in[l], lambda j: jnp.where(j >= f_blk0, j + f_blks, j), P_WIDTH,
                  BF16, tn=tn, name="proj")
        oa = _hgrn(p, f, lower_bound_table, hgrn_norm_w[l], bsz, seq, l)
        ob = _attn(p, lambda_q1[l], lambda_k1[l], lambda_q2[l], lambda_k2[l], subln_w[l],
                   bsz, seq, l)
        y = _merge(oa, ob, w_branch_a[l].astype(BF16), w_branch_b[l].astype(BF16), p)
        h2d = _out_proj(y, w_out[l].astype(BF16), h2d, final_w)
    return h2d.reshape(bsz, seq, d)
```

```python
import functools
import math

import jax
import jax.numpy as jnp
from jax import lax
from jax.experimental import pallas as pl
from jax.experimental.pallas import tpu as pltpu

D_MODEL = 4096
DEPTH = 1
assert DEPTH == 1, "the final rmsnorm is fused into the (single) layer's output projection"

A_KDIM = 128
A_VDIM = 128
A_WIDTH = D_MODEL // 2
A_HEADS = A_WIDTH // A_VDIM
A_KWIDTH = A_HEADS * A_KDIM
CHUNK = 64

B_HEAD_DIM = 128
B_WIDTH = D_MODEL // 2
B_HEADS = B_WIDTH // (2 * B_HEAD_DIM)
B_QK_WIDTH = B_HEADS * 2 * B_HEAD_DIM

NORM_EPS = 1e-6
SUBLN_EPS = 1e-5
NEG_INF = -1e30
LOG2E = math.log2(math.e)

F32 = jnp.float32
BF16 = jnp.bfloat16

V7X_VMEM_LIMIT_BYTES = 56 * 1024 * 1024

OFF_A_Q = 0
OFF_A_I = OFF_A_Q + A_KWIDTH
OFF_A_G = OFF_A_I + A_WIDTH
OFF_B_Q = OFF_A_G + A_WIDTH
OFF_B_K = OFF_B_Q + B_QK_WIDTH
OFF_B_V = OFF_B_K + B_QK_WIDTH
OFF_B_G = OFF_B_V + B_WIDTH
OFF_GATE_A = OFF_B_G + B_WIDTH
OFF_GATE_B = OFF_GATE_A + D_MODEL
P_WIDTH = OFF_GATE_B + D_MODEL
N_IN = P_WIDTH + A_KWIDTH

NT_DIMS = (((1,), (1,)), ((), ()))
TN_DIMS = (((0,), (0,)), ((), ()))


Q_SCALE = B_HEAD_DIM ** -0.5 * LOG2E


def _bf16_terms(x, n):
    terms = []
    for _ in range(n):
        t = x.astype(BF16).astype(F32)
        terms.append(t)
        x = x - t
    return terms


def _params(n_axes):
    return pltpu.CompilerParams(
        dimension_semantics=("arbitrary",) * n_axes,
        vmem_limit_bytes=V7X_VMEM_LIMIT_BYTES,
    )


def _rmsnorm_cast_kernel(x_ref, w_ref, o_ref):
    x = x_ref[...]
    ms = jnp.mean(x * x, axis=-1, keepdims=True)
    o_ref[...] = (x * lax.rsqrt(ms + NORM_EPS) * w_ref[...]).astype(o_ref.dtype)


def _rmsnorm_cast(x2d, w, rows=256):
    t, d = x2d.shape
    return pl.pallas_call(
        _rmsnorm_cast_kernel,
        grid=(t // rows,),
        in_specs=[pl.BlockSpec((rows, d), lambda i: (i, 0)),
                  pl.BlockSpec((1, d), lambda i: (0, 0))],
        out_specs=pl.BlockSpec((rows, d), lambda i: (i, 0)),
        out_shape=jax.ShapeDtypeStruct((t, d), BF16),
        compiler_params=_params(1),
        name="rmsnorm_cast",
    )(x2d, w.reshape(1, d))


def _proj_kernel(u_ref, w_ref, o_ref, wb_ref, *, scaled_blocks, scale):
    @pl.when(pl.program_id(1) == 0)
    def _():
        j = pl.program_id(0)
        lo, hi = scaled_blocks
        sc = jnp.where((j >= lo) & (j < hi), scale, 1.0).astype(F32)
        wb_ref[...] = (w_ref[...] * sc).astype(BF16)

    o_ref[...] = jnp.dot(u_ref[...], wb_ref[...],
                         preferred_element_type=F32).astype(o_ref.dtype)


def _proj(u, w, col_map, n_cols, out_dtype, tm=1024, tn=512, name="proj",
          scaled_cols=(0, 0), scale=1.0):
    t, d = u.shape
    kernel = functools.partial(
        _proj_kernel, scaled_blocks=(scaled_cols[0] // tn, scaled_cols[1] // tn), scale=scale)
    return pl.pallas_call(
        kernel,
        grid=(n_cols // tn, t // tm),
        in_specs=[pl.BlockSpec((tm, d), lambda j, i: (i, 0)),
                  pl.BlockSpec((d, tn), lambda j, i: (0, col_map(j)))],
        out_specs=pl.BlockSpec((tm, tn), lambda j, i: (i, j)),
        out_shape=jax.ShapeDtypeStruct((t, n_cols), out_dtype),
        scratch_shapes=[pltpu.VMEM((d, tn), BF16)],
        compiler_params=_params(2),
        name=name,
    )(u, w)


def _hgrn_kernel(q_ref, z_ref, i_ref, g_ref, lbt_ref, nw_ref, o_ref, *, layer, n_groups, group):
    rows = group * CHUNK
    t = lbt_ref[...]
    e = jnp.exp(t - jnp.max(t, axis=0, keepdims=True))
    lb = jnp.sum(e[:layer + 1], axis=0, keepdims=True) / jnp.sum(e, axis=0, keepdims=True)
    one_m_lb = 1.0 - lb
    nw = nw_ref[...]

    row = lax.broadcasted_iota(jnp.int32, (CHUNK, CHUNK), 0)
    col = lax.broadcasted_iota(jnp.int32, (CHUNK, CHUNK), 1)
    causal = row >= col
    tril = causal.astype(BF16)

    def body(n, st):
        r0 = pl.multiple_of(n * rows, rows)
        z = z_ref[pl.ds(r0, rows), :]
        ez = jnp.exp2(jnp.abs(z) * (-LOG2E))
        r = 1.0 / (1.0 + ez)
        big = one_m_lb * r
        small = ez * big
        pos = z >= 0.0
        log2_f = jnp.log2(lb + jnp.where(pos, big, small))
        kk = jnp.where(pos, small, big)

        h1 = log2_f.astype(BF16)
        h2 = (log2_f - h1.astype(F32)).astype(BF16)
        hs = jnp.concatenate([h1, h2], axis=1)
        bs = []
        for c in range(group):
            t2 = jnp.dot(tril, hs[c * CHUNK:(c + 1) * CHUNK], preferred_element_type=F32)
            bs.append(t2[:, :A_KDIM] + t2[:, A_KDIM:])
        b = jnp.concatenate(bs, axis=0)

        eb = jnp.exp2(b)
        q_dec = (q_ref[pl.ds(r0, rows), :].astype(F32) * eb).astype(BF16)
        k_inv32 = kk * (1.0 / eb)
        k_inv = k_inv32.astype(BF16)
        v = i_ref[pl.ds(r0, rows), :]

        o_intra, upd, decay = [], [], []
        for c in range(group):
            sl = slice(c * CHUNK, (c + 1) * CHUNK)
            scores = lax.dot_general(q_dec[sl], k_inv[sl], NT_DIMS, preferred_element_type=F32)
            scores = jnp.where(causal, scores, 0.0).astype(BF16)
            o_intra.append(jnp.dot(scores, v[sl], preferred_element_type=F32))
            d_c = jnp.exp2(bs[c][CHUNK - 1:CHUNK, :])
            k_end = (k_inv32[sl] * d_c).astype(BF16)
            upd.append(lax.dot_general(v[sl], k_end, TN_DIMS, preferred_element_type=F32))
            decay.append(d_c)

        outs = []
        for c in range(group):
            sl = slice(c * CHUNK, (c + 1) * CHUNK)
            outs.append(o_intra[c] + lax.dot_general(q_dec[sl], st.astype(BF16), NT_DIMS,
                                                     preferred_element_type=F32))
            st = st * decay[c] + upd[c]
        o = jnp.concatenate(outs, axis=0)

        ms = jnp.mean(o * o, axis=-1, keepdims=True)
        on = o * lax.rsqrt(ms + NORM_EPS) * nw
        g = g_ref[pl.ds(r0, rows), :].astype(F32)
        o_ref[pl.ds(r0, rows), :] = (on * (g * jax.nn.sigmoid(g))).astype(o_ref.dtype)
        return st

    lax.fori_loop(0, n_groups, body, jnp.zeros((A_VDIM, A_KDIM), F32))


def _hgrn(p, f, lb_table, norm_w, bsz, seq, layer, group=32):
    t = bsz * seq
    kernel = functools.partial(_hgrn_kernel, layer=layer, n_groups=seq // (CHUNK * group),
                               group=group)
    blk = lambda off: pl.BlockSpec((seq, A_KDIM), lambda b, h: (b, off // A_KDIM + h))
    return pl.pallas_call(
        kernel,
        grid=(bsz, A_HEADS),
        in_specs=[blk(OFF_A_Q),
                  pl.BlockSpec((seq, A_KDIM), lambda b, h: (b, h)),
                  blk(OFF_A_I),
                  blk(OFF_A_G),
                  pl.BlockSpec((DEPTH + 1, A_KDIM), lambda b, h: (0, h)),
                  pl.BlockSpec((1, A_VDIM), lambda b, h: (0, 0))],
        out_specs=pl.BlockSpec((seq, A_VDIM), lambda b, h: (b, h)),
        out_shape=jax.ShapeDtypeStruct((t, A_WIDTH), BF16),
        compiler_params=_params(2),
        name="hgrn2",
    )(p, f, p, p, lb_table, norm_w.reshape(1, A_VDIM))


def _attn_kernel(q_ref, k_ref, v_ref, g_ref, lq1_ref, lk1_ref, lq2_ref, lk2_ref, sw_ref,
                 o_ref, vt_ref, acc_ref, *, tq, tk, seq, lam_init, hpb):
    hb = pl.program_id(1)
    qi = pl.program_id(2)
    hd = B_HEAD_DIM
    w = 2 * hd

    @pl.when(qi == 0)
    def _():
        for hh in range(hpb):
            for c in range(seq // tk):
                vt_ref[hh, c] = (v_ref[c * tk:(c + 1) * tk, hh * w:(hh + 1) * w]
                                 .astype(F32).T.astype(BF16))

    lam = (jnp.exp(jnp.sum(lq1_ref[...] * lk1_ref[...], axis=-1, keepdims=True))
           - jnp.exp(jnp.sum(lq2_ref[...] * lk2_ref[...], axis=-1, keepdims=True))
           + lam_init)

    causal = (lax.broadcasted_iota(jnp.int32, (tk, tq), 1)
              >= lax.broadcasted_iota(jnp.int32, (tk, tq), 0))
    slope2 = [jnp.exp2(jnp.full((1, 1), -8.0 / B_HEADS, F32)
                       * (hb * hpb + hh + 1).astype(F32)) * LOG2E for hh in range(hpb)]

    lane_q = lax.broadcasted_iota(jnp.int32, (tq, hd), 1)
    lane_k = lax.broadcasted_iota(jnp.int32, (tk, hd), 1)
    q_loc = lax.broadcasted_iota(jnp.int32, (tq, hd), 0).astype(F32)
    k_loc = lax.broadcasted_iota(jnp.int32, (tk, hd), 0).astype(F32)
    qx, kx = [], []
    for s2 in slope2:
        s_1, s_2, s_3 = _bf16_terms(s2, 3)
        qx.append(jnp.where(lane_q == 0, s_1, jnp.where(lane_q == 1, s_2, jnp.where(
            lane_q == 2, s_3, jnp.where(lane_q < 6, q_loc, 0.0)))).astype(BF16))
        kx.append(jnp.where(lane_k < 3, k_loc, jnp.where(lane_k == 3, -s_1, jnp.where(
            lane_k == 4, -s_2, jnp.where(lane_k == 5, -s_3, 0.0)))).astype(BF16))

    chains = [(hh, mi) for hh in range(hpb) for mi in range(2)]
    qs = [jnp.concatenate([q_ref[:, hh * w + mi * hd: hh * w + (mi + 1) * hd], qx[hh]], axis=1)
          for hh, mi in chains]
    for ci in range(len(chains)):
        acc_ref[ci] = jnp.zeros((w, tq), F32)

    def scores(kb):
        k0 = pl.multiple_of(kb * tk, tk)
        out = []
        for ci, (hh, mi) in enumerate(chains):
            kblk = k_ref[pl.ds(k0, tk), hh * w + mi * hd: hh * w + (mi + 1) * hd]
            kaug = jnp.concatenate([kblk, kx[hh]], axis=1)
            out.append(lax.dot_general(kaug, qs[ci], NT_DIMS,
                                       preferred_element_type=F32))
        return tuple(out)

    def softmax_pv(kb, s_all, stats, masked):
        blk_dist = (qi * tq - kb * tk).astype(F32)
        new = []
        for ci, (hh, mi) in enumerate(chains):
            m, l = stats[2 * ci], stats[2 * ci + 1]
            s = s_all[ci]
            if masked:
                s = jnp.where(causal, s, NEG_INF)
            cb = -slope2[hh] * blk_dist
            m_new = jnp.maximum(m, jnp.max(s, axis=0, keepdims=True) + cb)
            alpha = jnp.exp2(m - m_new)
            p = jnp.exp2(s - (m_new - cb))
            l_new = alpha * l + jnp.sum(p, axis=0, keepdims=True)
            pv = jnp.dot(vt_ref[hh, kb], p.astype(BF16), preferred_element_type=F32)
            acc_ref[ci] = acc_ref[ci] * alpha + pv
            new += [m_new, l_new]
        return tuple(new)

    def body(kb, stats):
        return softmax_pv(kb, scores(kb), stats, False)

    stats0 = (jnp.full((1, tq), NEG_INF, F32), jnp.zeros((1, tq), F32)) * len(chains)
    stats = lax.fori_loop(0, qi, body, stats0)
    carry = softmax_pv(qi, scores(qi), stats, True)

    g = g_ref[...].astype(F32)
    gate = g * jax.nn.sigmoid(g)
    for hh in range(hpb):
        l0, l1 = carry[4 * hh + 1], carry[4 * hh + 3]
        ot = acc_ref[2 * hh] * (1.0 / l0) - acc_ref[2 * hh + 1] * (lam * (1.0 / l1))
        o = ot.T
        ms = jnp.mean(o * o, axis=-1, keepdims=True)
        on = o * lax.rsqrt(ms + SUBLN_EPS) * sw_ref[...] * (1.0 - lam_init)
        o_ref[:, hh * w:(hh + 1) * w] = (on * gate[:, hh * w:(hh + 1) * w]).astype(o_ref.dtype)


def _attn(p, lq1, lk1, lq2, lk2, subln_w, bsz, seq, layer, tq=256, tk=256, hpb=4):
    t = bsz * seq
    nq = seq // tq
    w = 2 * B_HEAD_DIM
    bw = hpb * w
    lam_init = 0.8 - 0.6 * math.exp(-0.3 * layer)
    kernel = functools.partial(_attn_kernel, tq=tq, tk=tk, seq=seq, lam_init=lam_init, hpb=hpb)
    vec = lambda n: pl.BlockSpec((1, n), lambda b, h, i: (0, 0))
    return pl.pallas_call(
        kernel,
        grid=(bsz, B_HEADS // hpb, nq),
        in_specs=[pl.BlockSpec((tq, bw), lambda b, h, i: (b * nq + i, OFF_B_Q // bw + h)),
                  pl.BlockSpec((seq, bw), lambda b, h, i: (b, OFF_B_K // bw + h)),
                  pl.BlockSpec((seq, bw), lambda b, h, i: (b, OFF_B_V // bw + h)),
                  pl.BlockSpec((tq, bw), lambda b, h, i: (b * nq + i, OFF_B_G // bw + h)),
                  vec(B_HEAD_DIM), vec(B_HEAD_DIM), vec(B_HEAD_DIM), vec(B_HEAD_DIM),
                  vec(w)],
        out_specs=pl.BlockSpec((tq, bw), lambda b, h, i: (b * nq + i, h)),
        out_shape=jax.ShapeDtypeStruct((t, B_WIDTH), BF16),
        scratch_shapes=[pltpu.VMEM((hpb, seq // tk, w, tk), BF16),
                        pltpu.VMEM((2 * hpb, w, tq), F32)],
        compiler_params=_params(3),
        name="diff_attn",
    )(p, p, p, p, lq1.reshape(1, -1), lk1.reshape(1, -1), lq2.reshape(1, -1),
      lk2.reshape(1, -1), subln_w.reshape(1, -1))


def _merge_kernel(oa_ref, ob_ref, wa_ref, wb_ref, ga_ref, gb_ref, y_ref, wa_bf, wb_bf):
    @pl.when(pl.program_id(1) == 0)
    def _():
        wa_bf[...] = wa_ref[...].astype(BF16)
        wb_bf[...] = wb_ref[...].astype(BF16)

    ya = jnp.dot(oa_ref[...], wa_bf[...], preferred_element_type=F32)
    yb = jnp.dot(ob_ref[...], wb_bf[...], preferred_element_type=F32)
    y = (jax.nn.sigmoid(ga_ref[...].astype(F32)) * ya
         + jax.nn.sigmoid(gb_ref[...].astype(F32)) * yb)
    y_ref[...] = y.astype(y_ref.dtype)


def _merge(oa, ob, wa, wb, p, tm=1024, tn=512):
    t = oa.shape[0]
    return pl.pallas_call(
        _merge_kernel,
        grid=(D_MODEL // tn, t // tm),
        in_specs=[pl.BlockSpec((tm, A_WIDTH), lambda j, i: (i, 0)),
                  pl.BlockSpec((tm, B_WIDTH), lambda j, i: (i, 0)),
                  pl.BlockSpec((A_WIDTH, tn), lambda j, i: (0, j)),
                  pl.BlockSpec((B_WIDTH, tn), lambda j, i: (0, j)),
                  pl.BlockSpec((tm, tn), lambda j, i: (i, OFF_GATE_A // tn + j)),
                  pl.BlockSpec((tm, tn), lambda j, i: (i, OFF_GATE_B // tn + j))],
        out_specs=pl.BlockSpec((tm, tn), lambda j, i: (i, j)),
        out_shape=jax.ShapeDtypeStruct((t, D_MODEL), BF16),
        scratch_shapes=[pltpu.VMEM((A_WIDTH, tn), BF16), pltpu.VMEM((B_WIDTH, tn), BF16)],
        compiler_params=_params(2),
        name="gated_merge",
    )(oa, ob, wa, wb, p, p)


def _out_kernel(y_ref, w_ref, x_ref, fw_ref, o_ref, h_ref, rs_ref, *, tn, nj):
    j = pl.program_id(1)

    @pl.when(j < nj)
    def _():
        c0 = pl.multiple_of(j * tn, tn)
        h_ref[:, pl.ds(c0, tn)] = x_ref[...] + jnp.dot(y_ref[...], w_ref[...],
                                                       preferred_element_type=F32)

    @pl.when(j == nj)
    def _():
        hres = h_ref[...]
        rs_ref[...] = lax.rsqrt(jnp.mean(hres * hres, axis=-1, keepdims=True) + NORM_EPS)

    @pl.when(j >= nj)
    def _():
        c0 = pl.multiple_of((j - nj) * tn, tn)
        o_ref[...] = h_ref[:, pl.ds(c0, tn)] * rs_ref[...] * fw_ref[:, pl.ds(c0, tn)]


def _out_proj(y, w, x2d, final_w, tm=1024, tn=512):
    t, d = x2d.shape
    nj = d // tn
    last = nj - 1
    return pl.pallas_call(
        functools.partial(_out_kernel, tn=tn, nj=nj),
        grid=(t // tm, 2 * nj),
        in_specs=[pl.BlockSpec((tm, d), lambda i, j: (i, 0)),
                  pl.BlockSpec((d, tn), lambda i, j: (0, jnp.minimum(j, last))),
                  pl.BlockSpec((tm, tn), lambda i, j: (i, jnp.minimum(j, last))),
                  pl.BlockSpec((1, d), lambda i, j: (0, 0))],
        out_specs=pl.BlockSpec((tm, tn), lambda i, j: (i, jnp.maximum(j - nj, 0))),
        out_shape=jax.ShapeDtypeStruct((t, d), F32),
        scratch_shapes=[pltpu.VMEM((tm, d), F32), pltpu.VMEM((tm, 1), F32)],
        compiler_params=_params(2),
        name="out_proj_norm",
    )(y, w, x2d, final_w.reshape(1, d))


def kernel(x, norm_w, w_in, lower_bound_table, hgrn_norm_w, lambda_q1, lambda_k1,
           lambda_q2, lambda_k2, subln_w, w_branch_a, w_branch_b, w_out, final_w):
    bsz, seq, d = x.shape
    h2d = x.reshape(bsz * seq, d)
    for l in range(DEPTH):
        u = _rmsnorm_cast(h2d, norm_w[l])
        tn = 512
        f_blk0 = A_KWIDTH // tn
        f_blks = A_KWIDTH // tn
        f = _proj(u, w_in[l], lambda j: j + f_blk0, A_KWIDTH, F32, tn=tn, name="proj_f")
        p = _proj(u, w_in[l], lambda j: jnp.where(j >= f_blk0, j + f_blks, j), P_WIDTH,
                  BF16, tn=tn, name="proj",
                  scaled_cols=(OFF_B_Q, OFF_B_Q + B_QK_WIDTH), scale=Q_SCALE)
        oa = _hgrn(p, f, lower_bound_table, hgrn_norm_w[l], bsz, seq, l)
        ob = _attn(p, lambda_q1[l], lambda_k1[l], lambda_q2[l], lambda_k2[l], subln_w[l],
                   bsz, seq, l)
        y = _merge(oa, ob, w_branch_a[l], w_branch_b[l], p)
        h2d = _out_proj(y, w_out[l].astype(BF16), h2d, final_w)
    return h2d.reshape(bsz, seq, d)
```

```python
import functools
import math

import jax
import jax.numpy as jnp
from jax import lax
from jax.experimental import pallas as pl
from jax.experimental.pallas import tpu as pltpu

D_MODEL = 4096
DEPTH = 1
assert DEPTH == 1, "the final rmsnorm is fused into the (single) layer's output projection"

A_KDIM = 128
A_VDIM = 128
A_WIDTH = D_MODEL // 2
A_HEADS = A_WIDTH // A_VDIM
A_KWIDTH = A_HEADS * A_KDIM
CHUNK = 64

B_HEAD_DIM = 128
B_WIDTH = D_MODEL // 2
B_HEADS = B_WIDTH // (2 * B_HEAD_DIM)
B_QK_WIDTH = B_HEADS * 2 * B_HEAD_DIM

NORM_EPS = 1e-6
SUBLN_EPS = 1e-5
NEG_INF = -1e30
LOG2E = math.log2(math.e)

F32 = jnp.float32
BF16 = jnp.bfloat16

V7X_VMEM_LIMIT_BYTES = 56 * 1024 * 1024

OFF_A_Q = 0
OFF_A_I = OFF_A_Q + A_KWIDTH
OFF_A_G = OFF_A_I + A_WIDTH
OFF_B_Q = OFF_A_G + A_WIDTH
OFF_B_K = OFF_B_Q + B_QK_WIDTH
OFF_B_V = OFF_B_K + B_QK_WIDTH
OFF_B_G = OFF_B_V + B_WIDTH
OFF_GATE_A = OFF_B_G + B_WIDTH
OFF_GATE_B = OFF_GATE_A + D_MODEL
P_WIDTH = OFF_GATE_B + D_MODEL
N_IN = P_WIDTH + A_KWIDTH

NT_DIMS = (((1,), (1,)), ((), ()))
TN_DIMS = (((0,), (0,)), ((), ()))


Q_SCALE = B_HEAD_DIM ** -0.5 * LOG2E


def _bf16_terms(x, n):
    terms = []
    for _ in range(n):
        t = x.astype(BF16).astype(F32)
        terms.append(t)
        x = x - t
    return terms


def _params(n_axes):
    return pltpu.CompilerParams(
        dimension_semantics=("arbitrary",) * n_axes,
        vmem_limit_bytes=V7X_VMEM_LIMIT_BYTES,
    )


def _rmsnorm_cast_kernel(x_ref, w_ref, o_ref):
    x = x_ref[...]
    ms = jnp.mean(x * x, axis=-1, keepdims=True)
    o_ref[...] = (x * lax.rsqrt(ms + NORM_EPS) * w_ref[...]).astype(o_ref.dtype)


def _rmsnorm_cast(x2d, w, rows=256):
    t, d = x2d.shape
    return pl.pallas_call(
        _rmsnorm_cast_kernel,
        grid=(t // rows,),
        in_specs=[pl.BlockSpec((rows, d), lambda i: (i, 0)),
                  pl.BlockSpec((1, d), lambda i: (0, 0))],
        out_specs=pl.BlockSpec((rows, d), lambda i: (i, 0)),
        out_shape=jax.ShapeDtypeStruct((t, d), BF16),
        compiler_params=_params(1),
        name="rmsnorm_cast",
    )(x2d, w.reshape(1, d))


def _proj_kernel(u_ref, w_hbm, o_ref, wst_ref, wb_ref, sem, *, tn, skip_from, skip_blocks,
                 scaled_blocks, scale):
    j = pl.program_id(0)
    nj = pl.num_programs(0)

    def w_copy(jj):
        cj = jj + jnp.where(jj >= skip_from, skip_blocks, 0)
        c0 = pl.multiple_of(cj * tn, tn)
        return pltpu.make_async_copy(w_hbm.at[:, pl.ds(c0, tn)], wst_ref, sem)

    @pl.when(pl.program_id(1) == 0)
    def _():
        @pl.when(j == 0)
        def _():
            w_copy(j).start()

        w_copy(j).wait()
        lo, hi = scaled_blocks
        sc = jnp.where((j >= lo) & (j < hi), scale, 1.0).astype(F32)
        wb_ref[...] = (wst_ref[...] * sc).astype(BF16)

        @pl.when(j + 1 < nj)
        def _():
            w_copy(j + 1).start()

    o_ref[...] = jnp.dot(u_ref[...], wb_ref[...],
                         preferred_element_type=F32).astype(o_ref.dtype)


def _proj(u, w, n_cols, out_dtype, skip_from, skip_blocks, tm=1024, tn=1024, name="proj",
          scaled_cols=(0, 0), scale=1.0):
    t, d = u.shape
    kernel = functools.partial(
        _proj_kernel, tn=tn, skip_from=skip_from, skip_blocks=skip_blocks,
        scaled_blocks=(scaled_cols[0] // tn, scaled_cols[1] // tn), scale=scale)
    return pl.pallas_call(
        kernel,
        grid=(n_cols // tn, t // tm),
        in_specs=[pl.BlockSpec((tm, d), lambda j, i: (i, 0)),
                  pl.BlockSpec(memory_space=pl.ANY)],
        out_specs=pl.BlockSpec((tm, tn), lambda j, i: (i, j)),
        out_shape=jax.ShapeDtypeStruct((t, n_cols), out_dtype),
        scratch_shapes=[pltpu.VMEM((d, tn), F32), pltpu.VMEM((d, tn), BF16),
                        pltpu.SemaphoreType.DMA(())],
        compiler_params=_params(2),
        name=name,
    )(u, w)


def _hgrn_kernel(q_ref, z_ref, i_ref, g_ref, lbt_ref, nw_ref, o_ref, *, layer, n_groups, group):
    rows = group * CHUNK
    t = lbt_ref[...]
    e = jnp.exp(t - jnp.max(t, axis=0, keepdims=True))
    lb = jnp.sum(e[:layer + 1], axis=0, keepdims=True) / jnp.sum(e, axis=0, keepdims=True)
    one_m_lb = 1.0 - lb
    nw = nw_ref[...]

    row = lax.broadcasted_iota(jnp.int32, (CHUNK, CHUNK), 0)
    col = lax.broadcasted_iota(jnp.int32, (CHUNK, CHUNK), 1)
    causal = row >= col
    tril = causal.astype(BF16)

    def body(n, st):
        r0 = pl.multiple_of(n * rows, rows)
        z = z_ref[pl.ds(r0, rows), :]
        ez = jnp.exp2(jnp.abs(z) * (-LOG2E))
        r = 1.0 / (1.0 + ez)
        big = one_m_lb * r
        small = ez * big
        pos = z >= 0.0
        log2_f = jnp.log2(lb + jnp.where(pos, big, small))
        kk = jnp.where(pos, small, big)

        h1 = log2_f.astype(BF16)
        h2 = (log2_f - h1.astype(F32)).astype(BF16)
        hs = jnp.concatenate([h1, h2], axis=1)
        bs = []
        for c in range(group):
            t2 = jnp.dot(tril, hs[c * CHUNK:(c + 1) * CHUNK], preferred_element_type=F32)
            bs.append(t2[:, :A_KDIM] + t2[:, A_KDIM:])
        b = jnp.concatenate(bs, axis=0)

        eb = jnp.exp2(b)
        q_dec = (q_ref[pl.ds(r0, rows), :].astype(F32) * eb).astype(BF16)
        k_inv32 = kk * (1.0 / eb)
        k_inv = k_inv32.astype(BF16)
        v = i_ref[pl.ds(r0, rows), :]

        o_intra, upd, decay = [], [], []
        for c in range(group):
            sl = slice(c * CHUNK, (c + 1) * CHUNK)
            scores = lax.dot_general(q_dec[sl], k_inv[sl], NT_DIMS, preferred_element_type=F32)
            scores = jnp.where(causal, scores, 0.0).astype(BF16)
            o_intra.append(jnp.dot(scores, v[sl], preferred_element_type=F32))
            d_c = jnp.exp2(bs[c][CHUNK - 1:CHUNK, :])
            k_end = (k_inv32[sl] * d_c).astype(BF16)
            upd.append(lax.dot_general(v[sl], k_end, TN_DIMS, preferred_element_type=F32))
            decay.append(d_c)

        outs = []
        for c in range(group):
            sl = slice(c * CHUNK, (c + 1) * CHUNK)
            outs.append(o_intra[c] + lax.dot_general(q_dec[sl], st.astype(BF16), NT_DIMS,
                                                     preferred_element_type=F32))
            st = st * decay[c] + upd[c]
        o = jnp.concatenate(outs, axis=0)

        ms = jnp.mean(o * o, axis=-1, keepdims=True)
        on = o * lax.rsqrt(ms + NORM_EPS) * nw
        g = g_ref[pl.ds(r0, rows), :].astype(F32)
        o_ref[pl.ds(r0, rows), :] = (on * (g * jax.nn.sigmoid(g))).astype(o_ref.dtype)
        return st

    lax.fori_loop(0, n_groups, body, jnp.zeros((A_VDIM, A_KDIM), F32))


def _hgrn(p, f, lb_table, norm_w, bsz, seq, layer, group=32):
    t = bsz * seq
    kernel = functools.partial(_hgrn_kernel, layer=layer, n_groups=seq // (CHUNK * group),
                               group=group)
    blk = lambda off: pl.BlockSpec((seq, A_KDIM), lambda b, h: (b, off // A_KDIM + h))
    return pl.pallas_call(
        kernel,
        grid=(bsz, A_HEADS),
        in_specs=[blk(OFF_A_Q),
                  pl.BlockSpec((seq, A_KDIM), lambda b, h: (b, h)),
                  blk(OFF_A_I),
                  blk(OFF_A_G),
                  pl.BlockSpec((DEPTH + 1, A_KDIM), lambda b, h: (0, h)),
                  pl.BlockSpec((1, A_VDIM), lambda b, h: (0, 0))],
        out_specs=pl.BlockSpec((seq, A_VDIM), lambda b, h: (b, h)),
        out_shape=jax.ShapeDtypeStruct((t, A_WIDTH), BF16),
        compiler_params=_params(2),
        name="hgrn2",
    )(p, f, p, p, lb_table, norm_w.reshape(1, A_VDIM))


def _attn_kernel(q_ref, k_ref, v_ref, g_ref, lq1_ref, lk1_ref, lq2_ref, lk2_ref, sw_ref,
                 o_ref, vt_ref, acc_ref, *, tq, tk, seq, lam_init, hpb):
    hb = pl.program_id(1)
    qi = pl.program_id(2)
    hd = B_HEAD_DIM
    w = 2 * hd

    @pl.when(qi == 0)
    def _():
        for hh in range(hpb):
            for c in range(seq // tk):
                vt_ref[hh, c] = (v_ref[c * tk:(c + 1) * tk, hh * w:(hh + 1) * w]
                                 .astype(F32).T.astype(BF16))

    lam = (jnp.exp(jnp.sum(lq1_ref[...] * lk1_ref[...], axis=-1, keepdims=True))
           - jnp.exp(jnp.sum(lq2_ref[...] * lk2_ref[...], axis=-1, keepdims=True))
           + lam_init)

    causal = (lax.broadcasted_iota(jnp.int32, (tk, tq), 1)
              >= lax.broadcasted_iota(jnp.int32, (tk, tq), 0))
    slope2 = [jnp.exp2(jnp.full((1, 1), -8.0 / B_HEADS, F32)
                       * (hb * hpb + hh + 1).astype(F32)) * LOG2E for hh in range(hpb)]

    lane_q = lax.broadcasted_iota(jnp.int32, (tq, hd), 1)
    lane_k = lax.broadcasted_iota(jnp.int32, (tk, hd), 1)
    q_loc = lax.broadcasted_iota(jnp.int32, (tq, hd), 0).astype(F32)
    k_loc = lax.broadcasted_iota(jnp.int32, (tk, hd), 0).astype(F32)
    qx, kx = [], []
    for s2 in slope2:
        s_1, s_2, s_3 = _bf16_terms(s2, 3)
        qx.append(jnp.where(lane_q == 0, s_1, jnp.where(lane_q == 1, s_2, jnp.where(
            lane_q == 2, s_3, jnp.where(lane_q < 6, q_loc, 0.0)))).astype(BF16))
        kx.append(jnp.where(lane_k < 3, k_loc, jnp.where(lane_k == 3, -s_1, jnp.where(
            lane_k == 4, -s_2, jnp.where(lane_k == 5, -s_3, 0.0)))).astype(BF16))

    chains = [(hh, mi) for hh in range(hpb) for mi in range(2)]
    qs = [jnp.concatenate([q_ref[:, hh * w + mi * hd: hh * w + (mi + 1) * hd], qx[hh]], axis=1)
          for hh, mi in chains]
    for ci in range(len(chains)):
        acc_ref[ci] = jnp.zeros((w, tq), F32)

    def scores(kb):
        k0 = pl.multiple_of(kb * tk, tk)
        out = []
        for ci, (hh, mi) in enumerate(chains):
            kblk = k_ref[pl.ds(k0, tk), hh * w + mi * hd: hh * w + (mi + 1) * hd]
            kaug = jnp.concatenate([kblk, kx[hh]], axis=1)
            out.append(lax.dot_general(kaug, qs[ci], NT_DIMS,
                                       preferred_element_type=F32))
        return tuple(out)

    def softmax_pv(kb, s_all, stats, masked):
        blk_dist = (qi * tq - kb * tk).astype(F32)
        new = []
        for ci, (hh, mi) in enumerate(chains):
            m, l = stats[2 * ci], stats[2 * ci + 1]
            s = s_all[ci]
            if masked:
                s = jnp.where(causal, s, NEG_INF)
            cb = -slope2[hh] * blk_dist
            m_new = jnp.maximum(m, jnp.max(s, axis=0, keepdims=True) + cb)
            alpha = jnp.exp2(m - m_new)
            p = jnp.exp2(s - (m_new - cb))
            l_new = alpha * l + jnp.sum(p, axis=0, keepdims=True)
            pv = jnp.dot(vt_ref[hh, kb], p.astype(BF16), preferred_element_type=F32)
            acc_ref[ci] = acc_ref[ci] * alpha + pv
            new += [m_new, l_new]
        return tuple(new)

    def body(kb, stats):
        return softmax_pv(kb, scores(kb), stats, False)

    stats0 = (jnp.full((1, tq), NEG_INF, F32), jnp.zeros((1, tq), F32)) * len(chains)
    stats = lax.fori_loop(0, qi, body, stats0)
    carry = softmax_pv(qi, scores(qi), stats, True)

    g = g_ref[...].astype(F32)
    gate = g * jax.nn.sigmoid(g)
    for hh in range(hpb):
        l0, l1 = carry[4 * hh + 1], carry[4 * hh + 3]
        ot = acc_ref[2 * hh] * (1.0 / l0) - acc_ref[2 * hh + 1] * (lam * (1.0 / l1))
        o = ot.T
        ms = jnp.mean(o * o, axis=-1, keepdims=True)
        on = o * lax.rsqrt(ms + SUBLN_EPS) * sw_ref[...] * (1.0 - lam_init)
        o_ref[:, hh * w:(hh + 1) * w] = (on * gate[:, hh * w:(hh + 1) * w]).astype(o_ref.dtype)


def _attn(p, lq1, lk1, lq2, lk2, subln_w, bsz, seq, layer, tq=256, tk=256, hpb=8):
    t = bsz * seq
    nq = seq // tq
    w = 2 * B_HEAD_DIM
    bw = hpb * w
    lam_init = 0.8 - 0.6 * math.exp(-0.3 * layer)
    kernel = functools.partial(_attn_kernel, tq=tq, tk=tk, seq=seq, lam_init=lam_init, hpb=hpb)
    vec = lambda n: pl.BlockSpec((1, n), lambda b, h, i: (0, 0))
    return pl.pallas_call(
        kernel,
        grid=(bsz, B_HEADS // hpb, nq),
        in_specs=[pl.BlockSpec((tq, bw), lambda b, h, i: (b * nq + i, OFF_B_Q // bw + h)),
                  pl.BlockSpec((seq, bw), lambda b, h, i: (b, OFF_B_K // bw + h)),
                  pl.BlockSpec((seq, bw), lambda b, h, i: (b, OFF_B_V // bw + h)),
                  pl.BlockSpec((tq, bw), lambda b, h, i: (b * nq + i, OFF_B_G // bw + h)),
                  vec(B_HEAD_DIM), vec(B_HEAD_DIM), vec(B_HEAD_DIM), vec(B_HEAD_DIM),
                  vec(w)],
        out_specs=pl.BlockSpec((tq, bw), lambda b, h, i: (b * nq + i, h)),
        out_shape=jax.ShapeDtypeStruct((t, B_WIDTH), BF16),
        scratch_shapes=[pltpu.VMEM((hpb, seq // tk, w, tk), BF16),
                        pltpu.VMEM((2 * hpb, w, tq), F32)],
        compiler_params=_params(3),
        name="diff_attn",
    )(p, p, p, p, lq1.reshape(1, -1), lk1.reshape(1, -1), lq2.reshape(1, -1),
      lk2.reshape(1, -1), subln_w.reshape(1, -1))


def _merge_kernel(oa_ref, ob_ref, wa_ref, wb_ref, ga_ref, gb_ref, y_ref, wa_bf, wb_bf):
    @pl.when(pl.program_id(1) == 0)
    def _():
        wa_bf[...] = wa_ref[...].astype(BF16)
        wb_bf[...] = wb_ref[...].astype(BF16)

    ya = jnp.dot(oa_ref[...], wa_bf[...], preferred_element_type=F32)
    yb = jnp.dot(ob_ref[...], wb_bf[...], preferred_element_type=F32)
    y = (jax.nn.sigmoid(ga_ref[...].astype(F32)) * ya
         + jax.nn.sigmoid(gb_ref[...].astype(F32)) * yb)
    y_ref[...] = y.astype(y_ref.dtype)


def _merge(oa, ob, wa, wb, p, tm=1024, tn=512):
    t = oa.shape[0]
    last_j = D_MODEL // tn - 1

    def w_idx(j, i):
        return (0, jnp.where(i >= 1, jnp.minimum(j + 1, last_j), j))

    return pl.pallas_call(
        _merge_kernel,
        grid=(D_MODEL // tn, t // tm),
        in_specs=[pl.BlockSpec((tm, A_WIDTH), lambda j, i: (i, 0)),
                  pl.BlockSpec((tm, B_WIDTH), lambda j, i: (i, 0)),
                  pl.BlockSpec((A_WIDTH, tn), w_idx),
                  pl.BlockSpec((B_WIDTH, tn), w_idx),
                  pl.BlockSpec((tm, tn), lambda j, i: (i, OFF_GATE_A // tn + j)),
                  pl.BlockSpec((tm, tn), lambda j, i: (i, OFF_GATE_B // tn + j))],
        out_specs=pl.BlockSpec((tm, tn), lambda j, i: (i, j)),
        out_shape=jax.ShapeDtypeStruct((t, D_MODEL), BF16),
        scratch_shapes=[pltpu.VMEM((A_WIDTH, tn), BF16), pltpu.VMEM((B_WIDTH, tn), BF16)],
        compiler_params=_params(2),
        name="gated_merge",
    )(oa, ob, wa, wb, p, p)


def _out_kernel(y_ref, w_ref, x_ref, fw_ref, o_ref, h_ref, rs_ref, *, tn, nj):
    j = pl.program_id(1)

    @pl.when(j < nj)
    def _():
        c0 = pl.multiple_of(j * tn, tn)
        h_ref[:, pl.ds(c0, tn)] = x_ref[...] + jnp.dot(y_ref[...], w_ref[...],
                                                       preferred_element_type=F32)

    @pl.when(j == nj)
    def _():
        hres = h_ref[...]
        rs_ref[...] = lax.rsqrt(jnp.mean(hres * hres, axis=-1, keepdims=True) + NORM_EPS)

    @pl.when(j >= nj)
    def _():
        c0 = pl.multiple_of((j - nj) * tn, tn)
        o_ref[...] = h_ref[:, pl.ds(c0, tn)] * rs_ref[...] * fw_ref[:, pl.ds(c0, tn)]


def _out_proj(y, w, x2d, final_w, tm=1024, tn=512):
    t, d = x2d.shape
    nj = d // tn
    last_i = t // tm - 1

    def next_i(i, j):
        return jnp.where(j >= nj, jnp.minimum(i + 1, last_i), i)

    def mm_j(j):
        return jnp.where(j >= nj, 0, j)

    return pl.pallas_call(
        functools.partial(_out_kernel, tn=tn, nj=nj),
        grid=(t // tm, 2 * nj),
        in_specs=[pl.BlockSpec((tm, d), lambda i, j: (next_i(i, j), 0)),
                  pl.BlockSpec((d, tn), lambda i, j: (0, mm_j(j))),
                  pl.BlockSpec((tm, tn), lambda i, j: (next_i(i, j), mm_j(j))),
                  pl.BlockSpec((1, d), lambda i, j: (0, 0))],
        out_specs=pl.BlockSpec((tm, tn), lambda i, j: (i, jnp.maximum(j - nj, 0))),
        out_shape=jax.ShapeDtypeStruct((t, d), F32),
        scratch_shapes=[pltpu.VMEM((tm, d), F32), pltpu.VMEM((tm, 1), F32)],
        compiler_params=_params(2),
        name="out_proj_norm",
    )(y, w, x2d, final_w.reshape(1, d))


def kernel(x, norm_w, w_in, lower_bound_table, hgrn_norm_w, lambda_q1, lambda_k1,
           lambda_q2, lambda_k2, subln_w, w_branch_a, w_branch_b, w_out, final_w):
    bsz, seq, d = x.shape
    h2d = x.reshape(bsz * seq, d)
    for l in range(DEPTH):
        u = _rmsnorm_cast(h2d, norm_w[l])
        tn = 1024
        f_blk0 = A_KWIDTH // tn
        f_blks = A_KWIDTH // tn
        f = _proj(u, w_in[l], A_KWIDTH, F32, 0, f_blk0, tn=tn, name="proj_f")
        p = _proj(u, w_in[l], P_WIDTH, BF16, f_blk0, f_blks, tn=tn, name="proj",
                  scaled_cols=(OFF_B_Q, OFF_B_Q + B_QK_WIDTH), scale=Q_SCALE)
        oa = _hgrn(p, f, lower_bound_table, hgrn_norm_w[l], bsz, seq, l)
        ob = _attn(p, lambda_q1[l], lambda_k1[l], lambda_q2[l], lambda_k2[l], subln_w[l],
                   bsz, seq, l)
        y = _merge(oa, ob, w_branch_a[l], w_branch_b[l], p)
        h2d = _out_proj(y, w_out[l].astype(BF16), h2d, final_w)
    return h2d.reshape(bsz, seq, d)
```

```python
import functools
import math

import jax
import jax.numpy as jnp
from jax import lax
from jax.experimental import pallas as pl
from jax.experimental.pallas import tpu as pltpu

D_MODEL = 4096
DEPTH = 1
assert DEPTH == 1, "the final rmsnorm is fused into the (single) layer's output projection"

A_KDIM = 128
A_VDIM = 128
A_WIDTH = D_MODEL // 2
A_HEADS = A_WIDTH // A_VDIM
A_KWIDTH = A_HEADS * A_KDIM
CHUNK = 64

B_HEAD_DIM = 128
B_WIDTH = D_MODEL // 2
B_HEADS = B_WIDTH // (2 * B_HEAD_DIM)
B_QK_WIDTH = B_HEADS * 2 * B_HEAD_DIM

NORM_EPS = 1e-6
SUBLN_EPS = 1e-5
NEG_INF = -1e30
LOG2E = math.log2(math.e)

F32 = jnp.float32
BF16 = jnp.bfloat16

V7X_VMEM_LIMIT_BYTES = 56 * 1024 * 1024

OFF_A_Q = 0
OFF_A_I = OFF_A_Q + A_KWIDTH
OFF_A_G = OFF_A_I + A_WIDTH
OFF_B_Q = OFF_A_G + A_WIDTH
OFF_B_K = OFF_B_Q + B_QK_WIDTH
OFF_B_V = OFF_B_K + B_QK_WIDTH
OFF_B_G = OFF_B_V + B_WIDTH
OFF_GATE_A = OFF_B_G + B_WIDTH
OFF_GATE_B = OFF_GATE_A + D_MODEL
P_WIDTH = OFF_GATE_B + D_MODEL
N_IN = P_WIDTH + A_KWIDTH

NT_DIMS = (((1,), (1,)), ((), ()))
TN_DIMS = (((0,), (0,)), ((), ()))


Q_SCALE = B_HEAD_DIM ** -0.5 * LOG2E


def _bf16_terms(x, n):
    terms = []
    for _ in range(n):
        t = x.astype(BF16).astype(F32)
        terms.append(t)
        x = x - t
    return terms


def _params(n_axes):
    return pltpu.CompilerParams(
        dimension_semantics=("arbitrary",) * n_axes,
        vmem_limit_bytes=V7X_VMEM_LIMIT_BYTES,
    )


def _rmsnorm_cast_kernel(x_ref, w_ref, o_ref):
    x = x_ref[...]
    ms = jnp.mean(x * x, axis=-1, keepdims=True)
    o_ref[...] = (x * lax.rsqrt(ms + NORM_EPS) * w_ref[...]).astype(o_ref.dtype)


def _rmsnorm_cast(x2d, w, rows=512):
    t, d = x2d.shape
    return pl.pallas_call(
        _rmsnorm_cast_kernel,
        grid=(t // rows,),
        in_specs=[pl.BlockSpec((rows, d), lambda i: (i, 0)),
                  pl.BlockSpec((1, d), lambda i: (0, 0))],
        out_specs=pl.BlockSpec((rows, d), lambda i: (i, 0)),
        out_shape=jax.ShapeDtypeStruct((t, d), BF16),
        compiler_params=_params(1),
        name="rmsnorm_cast",
    )(x2d, w.reshape(1, d))


def _proj_kernel(u_ref, w_hbm, o_ref, wst_ref, wb_ref, sem, *, tn, skip_from, skip_blocks,
                 scaled_blocks, scale):
    j = pl.program_id(0)
    nj = pl.num_programs(0)

    def w_copy(jj):
        cj = jj + jnp.where(jj >= skip_from, skip_blocks, 0)
        c0 = pl.multiple_of(cj * tn, tn)
        return pltpu.make_async_copy(w_hbm.at[:, pl.ds(c0, tn)], wst_ref, sem)

    @pl.when(pl.program_id(1) == 0)
    def _():
        @pl.when(j == 0)
        def _():
            w_copy(j).start()

        w_copy(j).wait()
        lo, hi = scaled_blocks
        sc = jnp.where((j >= lo) & (j < hi), scale, 1.0).astype(F32)
        wb_ref[...] = (wst_ref[...] * sc).astype(BF16)

        @pl.when(j + 1 < nj)
        def _():
            w_copy(j + 1).start()

    o_ref[...] = jnp.dot(u_ref[...], wb_ref[...],
                         preferred_element_type=F32).astype(o_ref.dtype)


def _proj(u, w, n_cols, out_dtype, skip_from, skip_blocks, tm=1024, tn=1024, name="proj",
          scaled_cols=(0, 0), scale=1.0):
    t, d = u.shape
    kernel = functools.partial(
        _proj_kernel, tn=tn, skip_from=skip_from, skip_blocks=skip_blocks,
        scaled_blocks=(scaled_cols[0] // tn, scaled_cols[1] // tn), scale=scale)
    return pl.pallas_call(
        kernel,
        grid=(n_cols // tn, t // tm),
        in_specs=[pl.BlockSpec((tm, d), lambda j, i: (i, 0)),
                  pl.BlockSpec(memory_space=pl.ANY)],
        out_specs=pl.BlockSpec((tm, tn), lambda j, i: (i, j)),
        out_shape=jax.ShapeDtypeStruct((t, n_cols), out_dtype),
        scratch_shapes=[pltpu.VMEM((d, tn), F32), pltpu.VMEM((d, tn), BF16),
                        pltpu.SemaphoreType.DMA(())],
        compiler_params=_params(2),
        name=name,
    )(u, w)


def _hgrn_kernel(q_ref, z_ref, i_ref, g_ref, lbt_ref, nw_ref, o_ref, *, layer, n_groups, group):
    rows = group * CHUNK
    t = lbt_ref[...]
    e = jnp.exp(t - jnp.max(t, axis=0, keepdims=True))
    lb = jnp.sum(e[:layer + 1], axis=0, keepdims=True) / jnp.sum(e, axis=0, keepdims=True)
    one_m_lb = 1.0 - lb
    nw = nw_ref[...]

    row = lax.broadcasted_iota(jnp.int32, (CHUNK, CHUNK), 0)
    col = lax.broadcasted_iota(jnp.int32, (CHUNK, CHUNK), 1)
    causal = row >= col
    tril = causal.astype(BF16)

    def body(n, st):
        r0 = pl.multiple_of(n * rows, rows)
        z = z_ref[pl.ds(r0, rows), :]
        ez = jnp.exp2(jnp.abs(z) * (-LOG2E))
        r = 1.0 / (1.0 + ez)
        big = one_m_lb * r
        small = ez * big
        pos = z >= 0.0
        log2_f = jnp.log2(lb + jnp.where(pos, big, small))
        kk = jnp.where(pos, small, big)

        h1 = log2_f.astype(BF16)
        h2 = (log2_f - h1.astype(F32)).astype(BF16)
        hs = jnp.concatenate([h1, h2], axis=1)
        bs = []
        for c in range(group):
            t2 = jnp.dot(tril, hs[c * CHUNK:(c + 1) * CHUNK], preferred_element_type=F32)
            bs.append(t2[:, :A_KDIM] + t2[:, A_KDIM:])
        b = jnp.concatenate(bs, axis=0)

        eb = jnp.exp2(b)
        q_dec = (q_ref[pl.ds(r0, rows), :].astype(F32) * eb).astype(BF16)
        k_inv32 = kk * (1.0 / eb)
        k_inv = k_inv32.astype(BF16)
        v = i_ref[pl.ds(r0, rows), :]

        o_intra, upd, decay = [], [], []
        for c in range(group):
            sl = slice(c * CHUNK, (c + 1) * CHUNK)
            scores = lax.dot_general(q_dec[sl], k_inv[sl], NT_DIMS, preferred_element_type=F32)
            scores = jnp.where(causal, scores, 0.0).astype(BF16)
            o_intra.append(jnp.dot(scores, v[sl], preferred_element_type=F32))
            d_c = jnp.exp2(bs[c][CHUNK - 1:CHUNK, :])
            k_end = (k_inv32[sl] * d_c).astype(BF16)
            upd.append(lax.dot_general(v[sl], k_end, TN_DIMS, preferred_element_type=F32))
            decay.append(d_c)

        outs = []
        for c in range(group):
            sl = slice(c * CHUNK, (c + 1) * CHUNK)
            outs.append(o_intra[c] + lax.dot_general(q_dec[sl], st.astype(BF16), NT_DIMS,
                                                     preferred_element_type=F32))
            st = st * decay[c] + upd[c]
        o = jnp.concatenate(outs, axis=0)

        ms = jnp.mean(o * o, axis=-1, keepdims=True)
        on = o * lax.rsqrt(ms + NORM_EPS) * nw
        g = g_ref[pl.ds(r0, rows), :].astype(F32)
        o_ref[pl.ds(r0, rows), :] = (on * (g * jax.nn.sigmoid(g))).astype(o_ref.dtype)
        return st

    lax.fori_loop(0, n_groups, body, jnp.zeros((A_VDIM, A_KDIM), F32))


def _hgrn(p, f, lb_table, norm_w, bsz, seq, layer, group=32):
    t = bsz * seq
    kernel = functools.partial(_hgrn_kernel, layer=layer, n_groups=seq // (CHUNK * group),
                               group=group)
    blk = lambda off: pl.BlockSpec((seq, A_KDIM), lambda b, h: (b, off // A_KDIM + h))
    return pl.pallas_call(
        kernel,
        grid=(bsz, A_HEADS),
        in_specs=[blk(OFF_A_Q),
                  pl.BlockSpec((seq, A_KDIM), lambda b, h: (b, h)),
                  blk(OFF_A_I),
                  blk(OFF_A_G),
                  pl.BlockSpec((DEPTH + 1, A_KDIM), lambda b, h: (0, h)),
                  pl.BlockSpec((1, A_VDIM), lambda b, h: (0, 0))],
        out_specs=pl.BlockSpec((seq, A_VDIM), lambda b, h: (b, h)),
        out_shape=jax.ShapeDtypeStruct((t, A_WIDTH), BF16),
        compiler_params=_params(2),
        name="hgrn2",
    )(p, f, p, p, lb_table, norm_w.reshape(1, A_VDIM))


def _attn_kernel(q_ref, k_ref, v_ref, g_ref, lq1_ref, lk1_ref, lq2_ref, lk2_ref, sw_ref,
                 o_ref, vt_ref, acc_ref, *, tq, tk, seq, lam_init, hpb):
    hb = pl.program_id(1)
    qi = pl.program_id(2)
    hd = B_HEAD_DIM
    w = 2 * hd

    @pl.when(qi == 0)
    def _():
        for hh in range(hpb):
            for c in range(seq // tk):
                vt_ref[hh, c] = (v_ref[c * tk:(c + 1) * tk, hh * w:(hh + 1) * w]
                                 .astype(F32).T.astype(BF16))

    lam = (jnp.exp(jnp.sum(lq1_ref[...] * lk1_ref[...], axis=-1, keepdims=True))
           - jnp.exp(jnp.sum(lq2_ref[...] * lk2_ref[...], axis=-1, keepdims=True))
           + lam_init)

    causal = (lax.broadcasted_iota(jnp.int32, (tk, tq), 1)
              >= lax.broadcasted_iota(jnp.int32, (tk, tq), 0))
    slope2 = [jnp.exp2(jnp.full((1, 1), -8.0 / B_HEADS, F32)
                       * (hb * hpb + hh + 1).astype(F32)) * LOG2E for hh in range(hpb)]

    lane_q = lax.broadcasted_iota(jnp.int32, (tq, hd), 1)
    lane_k = lax.broadcasted_iota(jnp.int32, (tk, hd), 1)
    q_loc = lax.broadcasted_iota(jnp.int32, (tq, hd), 0).astype(F32)
    k_loc = lax.broadcasted_iota(jnp.int32, (tk, hd), 0).astype(F32)
    qx, kx = [], []
    for s2 in slope2:
        s_1, s_2, s_3 = _bf16_terms(s2, 3)
        qx.append(jnp.where(lane_q == 0, s_1, jnp.where(lane_q == 1, s_2, jnp.where(
            lane_q == 2, s_3, jnp.where(lane_q < 6, q_loc, 0.0)))).astype(BF16))
        kx.append(jnp.where(lane_k < 3, k_loc, jnp.where(lane_k == 3, -s_1, jnp.where(
            lane_k == 4, -s_2, jnp.where(lane_k == 5, -s_3, 0.0)))).astype(BF16))

    chains = [(hh, mi) for hh in range(hpb) for mi in range(2)]
    qs = [jnp.concatenate([q_ref[:, hh * w + mi * hd: hh * w + (mi + 1) * hd], qx[hh]], axis=1)
          for hh, mi in chains]
    for ci in range(len(chains)):
        acc_ref[ci] = jnp.zeros((w, tq), F32)

    def scores(kb):
        k0 = pl.multiple_of(kb * tk, tk)
        out = []
        for ci, (hh, mi) in enumerate(chains):
            kblk = k_ref[pl.ds(k0, tk), hh * w + mi * hd: hh * w + (mi + 1) * hd]
            kaug = jnp.concatenate([kblk, kx[hh]], axis=1)
            out.append(lax.dot_general(kaug, qs[ci], NT_DIMS,
                                       preferred_element_type=F32))
        return tuple(out)

    def softmax_pv(kb, s_all, stats, masked):
        blk_dist = (qi * tq - kb * tk).astype(F32)
        new = []
        for ci, (hh, mi) in enumerate(chains):
            m, l = stats[2 * ci], stats[2 * ci + 1]
            s = s_all[ci]
            if masked:
                s = jnp.where(causal, s, NEG_INF)
            cb = -slope2[hh] * blk_dist
            m_new = jnp.maximum(m, jnp.max(s, axis=0, keepdims=True) + cb)
            alpha = jnp.exp2(m - m_new)
            p = jnp.exp2(s - (m_new - cb))
            l_new = alpha * l + jnp.sum(p, axis=0, keepdims=True)
            pv = jnp.dot(vt_ref[hh, kb], p.astype(BF16), preferred_element_type=F32)
            acc_ref[ci] = acc_ref[ci] * alpha + pv
            new += [m_new, l_new]
        return tuple(new)

    def body(kb, stats):
        return softmax_pv(kb, scores(kb), stats, False)

    stats0 = (jnp.full((1, tq), NEG_INF, F32), jnp.zeros((1, tq), F32)) * len(chains)
    stats = lax.fori_loop(0, qi, body, stats0)
    carry = softmax_pv(qi, scores(qi), stats, True)

    g = g_ref[...].astype(F32)
    gate = g * jax.nn.sigmoid(g)
    gain = sw_ref[...] * (1.0 - lam_init)
    for hh in range(hpb):
        l0, l1 = carry[4 * hh + 1], carry[4 * hh + 3]
        ot = acc_ref[2 * hh] * (1.0 / l0) - acc_ref[2 * hh + 1] * (lam * (1.0 / l1))
        o = ot.T
        ms = jnp.mean(o * o, axis=-1, keepdims=True)
        on = o * lax.rsqrt(ms + SUBLN_EPS) * gain
        o_ref[:, hh * w:(hh + 1) * w] = (on * gate[:, hh * w:(hh + 1) * w]).astype(o_ref.dtype)


def _attn(p, lq1, lk1, lq2, lk2, subln_w, bsz, seq, layer, tq=256, tk=256, hpb=8):
    t = bsz * seq
    nq = seq // tq
    w = 2 * B_HEAD_DIM
    bw = hpb * w
    lam_init = 0.8 - 0.6 * math.exp(-0.3 * layer)
    kernel = functools.partial(_attn_kernel, tq=tq, tk=tk, seq=seq, lam_init=lam_init, hpb=hpb)
    vec = lambda n: pl.BlockSpec((1, n), lambda b, h, i: (0, 0))
    return pl.pallas_call(
        kernel,
        grid=(bsz, B_HEADS // hpb, nq),
        in_specs=[pl.BlockSpec((tq, bw), lambda b, h, i: (b * nq + i, OFF_B_Q // bw + h)),
                  pl.BlockSpec((seq, bw), lambda b, h, i: (b, OFF_B_K // bw + h)),
                  pl.BlockSpec((seq, bw), lambda b, h, i: (b, OFF_B_V // bw + h)),
                  pl.BlockSpec((tq, bw), lambda b, h, i: (b * nq + i, OFF_B_G // bw + h)),
                  vec(B_HEAD_DIM), vec(B_HEAD_DIM), vec(B_HEAD_DIM), vec(B_HEAD_DIM),
                  vec(w)],
        out_specs=pl.BlockSpec((tq, bw), lambda b, h, i: (b * nq + i, h)),
        out_shape=jax.ShapeDtypeStruct((t, B_WIDTH), BF16),
        scratch_shapes=[pltpu.VMEM((hpb, seq // tk, w, tk), BF16),
                        pltpu.VMEM((2 * hpb, w, tq), F32)],
        compiler_params=_params(3),
        name="diff_attn",
    )(p, p, p, p, lq1.reshape(1, -1), lk1.reshape(1, -1), lq2.reshape(1, -1),
      lk2.reshape(1, -1), subln_w.reshape(1, -1))


def _merge_kernel(oa_ref, ob_ref, wa_ref, wb_ref, ga_ref, gb_ref, wo_ref, y_ref, wo_bf_ref,
                  wa_bf, wb_bf):
    @pl.when(pl.program_id(1) == 0)
    def _():
        wa_bf[...] = wa_ref[...].astype(BF16)
        wb_bf[...] = wb_ref[...].astype(BF16)

    wo_bf_ref[...] = wo_ref[...].astype(BF16)

    ya = jnp.dot(oa_ref[...], wa_bf[...], preferred_element_type=F32)
    yb = jnp.dot(ob_ref[...], wb_bf[...], preferred_element_type=F32)
    y = (jax.nn.sigmoid(ga_ref[...].astype(F32)) * ya
         + jax.nn.sigmoid(gb_ref[...].astype(F32)) * yb)
    y_ref[...] = y.astype(y_ref.dtype)


def _merge(oa, ob, wa, wb, p, w_out, tm=1024, tn=512):
    t = oa.shape[0]
    last_j = D_MODEL // tn - 1
    ni = t // tm
    wo_rows = w_out.shape[0] // ((D_MODEL // tn) * ni)
    wo_spec = pl.BlockSpec((wo_rows, w_out.shape[1]), lambda j, i: (j * ni + i, 0))

    def w_idx(j, i):
        return (0, jnp.where(i >= 1, jnp.minimum(j + 1, last_j), j))

    return pl.pallas_call(
        _merge_kernel,
        grid=(D_MODEL // tn, t // tm),
        in_specs=[pl.BlockSpec((tm, A_WIDTH), lambda j, i: (i, 0)),
                  pl.BlockSpec((tm, B_WIDTH), lambda j, i: (i, 0)),
                  pl.BlockSpec((A_WIDTH, tn), w_idx),
                  pl.BlockSpec((B_WIDTH, tn), w_idx),
                  pl.BlockSpec((tm, tn), lambda j, i: (i, OFF_GATE_A // tn + j)),
                  pl.BlockSpec((tm, tn), lambda j, i: (i, OFF_GATE_B // tn + j)),
                  wo_spec],
        out_specs=[pl.BlockSpec((tm, tn), lambda j, i: (i, j)), wo_spec],
        out_shape=[jax.ShapeDtypeStruct((t, D_MODEL), BF16),
                   jax.ShapeDtypeStruct(w_out.shape, BF16)],
        scratch_shapes=[pltpu.VMEM((A_WIDTH, tn), BF16), pltpu.VMEM((B_WIDTH, tn), BF16)],
        compiler_params=_params(2),
        name="gated_merge",
    )(oa, ob, wa, wb, p, p, w_out)


def _out_kernel(y_ref, w_ref, x_ref, fw_ref, o_ref, h_ref, rs_ref, *, tn, nj):
    j = pl.program_id(1)

    @pl.when(j < nj)
    def _():
        c0 = pl.multiple_of(j * tn, tn)
        h_ref[:, pl.ds(c0, tn)] = x_ref[...] + jnp.dot(y_ref[...], w_ref[...],
                                                       preferred_element_type=F32)

    @pl.when(j == nj)
    def _():
        hres = h_ref[...]
        rs_ref[...] = lax.rsqrt(jnp.mean(hres * hres, axis=-1, keepdims=True) + NORM_EPS)

    @pl.when(j >= nj)
    def _():
        to = o_ref.shape[1]
        c0 = pl.multiple_of((j - nj) * to, to)
        o_ref[...] = h_ref[:, pl.ds(c0, to)] * rs_ref[...] * fw_ref[:, pl.ds(c0, to)]


def _out_proj(y, w, x2d, final_w, tm=1024, tn=512, to=1024):
    t, d = x2d.shape
    nj = d // tn
    last_i = t // tm - 1

    def next_i(i, j):
        return jnp.where(j >= nj, jnp.minimum(i + 1, last_i), i)

    def mm_j(j):
        return jnp.where(j >= nj, 0, j)

    return pl.pallas_call(
        functools.partial(_out_kernel, tn=tn, nj=nj),
        grid=(t // tm, nj + d // to),
        in_specs=[pl.BlockSpec((tm, d), lambda i, j: (next_i(i, j), 0)),
                  pl.BlockSpec((d, tn), lambda i, j: (0, mm_j(j))),
                  pl.BlockSpec((tm, tn), lambda i, j: (next_i(i, j), mm_j(j))),
                  pl.BlockSpec((1, d), lambda i, j: (0, 0))],
        out_specs=pl.BlockSpec((tm, to), lambda i, j: (i, jnp.maximum(j - nj, 0))),
        out_shape=jax.ShapeDtypeStruct((t, d), F32),
        scratch_shapes=[pltpu.VMEM((tm, d), F32), pltpu.VMEM((tm, 1), F32)],
        compiler_params=_params(2),
        name="out_proj_norm",
    )(y, w, x2d, final_w.reshape(1, d))


def kernel(x, norm_w, w_in, lower_bound_table, hgrn_norm_w, lambda_q1, lambda_k1,
           lambda_q2, lambda_k2, subln_w, w_branch_a, w_branch_b, w_out, final_w):
    bsz, seq, d = x.shape
    h2d = x.reshape(bsz * seq, d)
    for l in range(DEPTH):
        u = _rmsnorm_cast(h2d, norm_w[l])
        tn = 1024
        f_blk0 = A_KWIDTH // tn
        f_blks = A_KWIDTH // tn
        f = _proj(u, w_in[l], A_KWIDTH, F32, 0, f_blk0, tn=tn, name="proj_f")
        p = _proj(u, w_in[l], P_WIDTH, BF16, f_blk0, f_blks, tn=tn, name="proj",
                  scaled_cols=(OFF_B_Q, OFF_B_Q + B_QK_WIDTH), scale=Q_SCALE)
        oa = _hgrn(p, f, lower_bound_table, hgrn_norm_w[l], bsz, seq, l)
        ob = _attn(p, lambda_q1[l], lambda_k1[l], lambda_q2[l], lambda_k2[l], subln_w[l],
                   bsz, seq, l)
        y, w_out_bf = _merge(oa, ob, w_branch_a[l], w_branch_b[l], p, w_out[l])
        h2d = _out_proj(y, w_out_bf, h2d, final_w)
    return h2d.reshape(bsz, seq, d)
```
